```python
import jax, jax.numpy as jnp
from jax import lax
import numpy as np

D_MODEL = 2048
BATCH = 1
SEQ = 8192
DEPTH = 1
DEC_BATCH = 128
DEC_SEQ = 4
PAST_LEN = 2048
PAGE_SIZE = 128

HEAD_DIM = 128
HEADS_PER_GROUP = 4
DILATED_GROUPS = ((128, 1), (512, 4), (2048, 16))
N_GROUPS = len(DILATED_GROUPS)
N_ATTN_HEADS = N_GROUPS * HEADS_PER_GROUP
ATTN_WIDTH = N_ATTN_HEADS * HEAD_DIM
CONV_CH = D_MODEL - ATTN_WIDTH
CONV_WIDTH = 31
IN_WIDTH = 3 * ATTN_WIDTH + 2 * CONV_CH
MEM_TOKENS = 256
MEM_HEADS = 4
MEM_WIDTH = MEM_HEADS * HEAD_DIM
PEER_HEADS = 8
PEER_KEY_DIM = 256
N_KEYS = 128
N_EXPERTS = N_KEYS * N_KEYS
PEER_TOPK = 16
PEER_BLOCK = 128
Q_BLOCK = 128
ROPE_THETA = 10000.0
EPS = 1e-6

kernel_name = 'hybrid_dilated_conformer_peer_step'


def rms_norm(x, w):
    xf = x.astype(jnp.float32)
    y = xf * lax.rsqrt(jnp.mean(xf * xf, axis=-1, keepdims=True) + EPS)
    return (y * w.astype(jnp.float32)).astype(x.dtype)


def layer_norm(x, w, b):
    xf = x.astype(jnp.float32)
    mu = jnp.mean(xf, axis=-1, keepdims=True)
    xc = xf - mu
    y = xc * lax.rsqrt(jnp.mean(xc * xc, axis=-1, keepdims=True) + EPS)
    return (y * w.astype(jnp.float32) + b.astype(jnp.float32)).astype(x.dtype)


def rope(x, pos):
    half = HEAD_DIM // 2
    inv = ROPE_THETA ** (-jnp.arange(half, dtype=jnp.float32) / half)
    ang = pos.astype(jnp.float32)[:, None] * inv[None, :]
    cos = jnp.cos(ang)[:, None, :]
    sin = jnp.sin(ang)[:, None, :]
    xf = x.astype(jnp.float32)
    x1, x2 = xf[..., :half], xf[..., half:]
    return jnp.concatenate([x1 * cos - x2 * sin, x2 * cos + x1 * sin], axis=-1).astype(x.dtype)


def mixer_projections(h, w_in, q_norm_w, k_norm_w, pos):
    N, S, _ = h.shape
    z = h @ w_in
    q, k, v, glu = jnp.split(z, [ATTN_WIDTH, 2 * ATTN_WIDTH, 3 * ATTN_WIDTH], axis=-1)
    q = rope(rms_norm(q.reshape(N, S, N_ATTN_HEADS, HEAD_DIM), q_norm_w), pos)
    k = rope(rms_norm(k.reshape(N, S, N_ATTN_HEADS, HEAD_DIM), k_norm_w), pos)
    v = v.reshape(N, S, N_ATTN_HEADS, HEAD_DIM)
    a, gate = jnp.split(glu, 2, axis=-1)
    u = a * jax.nn.sigmoid(gate)
    return q, k, v, u


def dilated_attn_prompt(q, k, v, dil, n_back):
    N, S, H, hd = q.shape
    L = S // dil
    Lp = -(-L // Q_BLOCK) * Q_BLOCK
    nb = Lp // Q_BLOCK
    NP = N * dil

    def to_phases(t):
        t = t.reshape(N, L, dil, H, hd).transpose(0, 2, 1, 3, 4).reshape(NP, L, H, hd)
        return jnp.pad(t, ((0, 0), (0, Lp - L), (0, 0), (0, 0)))

    def band(t):
        tb = jnp.pad(t, ((0, 0), (Q_BLOCK, 0), (0, 0), (0, 0))).reshape(NP, nb + 1, Q_BLOCK, H, hd)
        return jnp.concatenate([tb[:, :-1], tb[:, 1:]], axis=2)

    qb = to_phases(q).reshape(NP, nb, Q_BLOCK, H, hd)
    kb = band(to_phases(k))
    vb = band(to_phases(v))
    s = jnp.einsum('nbqhd,nbkhd->nbhqk', qb, kb).astype(jnp.float32) * (hd ** -0.5)
    qi = jnp.arange(Q_BLOCK)[:, None]
    kj = jnp.arange(2 * Q_BLOCK)[None, :]
    dist = qi + Q_BLOCK - kj
    blk = jnp.arange(nb)[:, None, None]
    valid = (dist >= 0) & (dist <= n_back) & (blk * Q_BLOCK + kj - Q_BLOCK >= 0)
    s = jnp.where(valid[None, :, None], s, -jnp.inf)
    m = jnp.max(s, axis=-1, keepdims=True)
    p = jnp.exp(s - m)
    den = jnp.sum(p, axis=-1, keepdims=True)
    o = jnp.einsum('nbhqk,nbkhd->nbqhd', p / den, vb.astype(jnp.float32))
    lse = (m + jnp.log(den))[..., 0].transpose(0, 1, 3, 2)
    o = o.reshape(N, dil, Lp, H, hd)[:, :, :L].transpose(0, 2, 1, 3, 4).reshape(N, S, H, hd)
    lse = lse.reshape(N, dil, Lp, H)[:, :, :L].transpose(0, 2, 1, 3).reshape(N, S, H)
    return o, lse


def dilated_attn_sample(q, k_buf, v_buf, k_new, v_new, dil, n_back):
    N, T, H, hd = q.shape
    wb = k_buf.shape[1]
    k_all = jnp.concatenate([k_buf.astype(k_new.dtype), k_new], axis=1)
    v_all = jnp.concatenate([v_buf.astype(v_new.dtype), v_new], axis=1)
    steps = jnp.arange(n_back + 1)
    idx = wb + jnp.arange(T)[:, None] - steps[None, :] * dil
    valid = idx >= 0
    idx = jnp.maximum(idx, 0)
    kg = k_all[:, idx]
    vg = v_all[:, idx]
    s = jnp.einsum('nthd,ntkhd->nhtk', q, kg).astype(jnp.float32) * (hd ** -0.5)
    s = jnp.where(valid[None, None], s, -jnp.inf)
    m = jnp.max(s, axis=-1, keepdims=True)
    p = jnp.exp(s - m)
    den = jnp.sum(p, axis=-1, keepdims=True)
    o = jnp.einsum('nhtk,ntkhd->nthd', p / den, vg.astype(jnp.float32))
    lse = (m + jnp.log(den))[..., 0].transpose(0, 2, 1)
    return o, lse, k_all, v_all


def combine_groups(outs, lses, dtype):
    alpha = jax.nn.softmax(jnp.stack(lses, axis=0), axis=0)
    o = jnp.concatenate([a[..., None] * og for a, og in zip(alpha, outs)], axis=2)
    N, S = o.shape[:2]
    return o.reshape(N, S, ATTN_WIDTH).astype(dtype)


def conv_tail(u_padded, conv_w, conv_b, ln_w, ln_b):
    y = lax.conv_general_dilated(u_padded, conv_w[:, None, :].astype(u_padded.dtype), (1,), 'VALID',
                                 dimension_numbers=('NWC', 'WIO', 'NWC'), feature_group_count=CONV_CH)
    y = layer_norm(y + conv_b.astype(y.dtype), ln_w, ln_b)
    return jax.nn.silu(y)


def memory_kv(mem, mem_norm_w, wk, wv, kn_w):
    N, M, _ = mem.shape
    mh = rms_norm(mem, mem_norm_w)
    k = rms_norm((mh @ wk).reshape(N, M, MEM_HEADS, HEAD_DIM), kn_w)
    v = (mh @ wv).reshape(N, M, MEM_HEADS, HEAD_DIM)
    return k, v


def memory_attend(h, k, v, wq, qn_w, wo):
    N, S, _ = h.shape
    q = rms_norm((h @ wq).reshape(N, S, MEM_HEADS, HEAD_DIM), qn_w)
    s = jnp.einsum('nshd,nmhd->nhsm', q, k.astype(q.dtype)).astype(jnp.float32) * (HEAD_DIM ** -0.5)
    p = jax.nn.softmax(s, axis=-1)
    o = jnp.einsum('nhsm,nmhd->nshd', p, v.astype(jnp.float32)).astype(h.dtype)
    return o.reshape(N, S, MEM_WIDTH) @ wo


def peer_tokens(h, w_q, keys1, keys2, peer_u, peer_v):
    T = h.shape[0]
    half = PEER_KEY_DIM // 2
    q = (h @ w_q).reshape(T, PEER_HEADS, PEER_KEY_DIM).astype(jnp.float32)
    s1 = jnp.einsum('thd,hkd->thk', q[..., :half], keys1.astype(jnp.float32))
    s2 = jnp.einsum('thd,hkd->thk', q[..., half:], keys2.astype(jnp.float32))
    v1, i1 = lax.top_k(s1, PEER_TOPK)
    v2, i2 = lax.top_k(s2, PEER_TOPK)
    cand = (v1[..., :, None] + v2[..., None, :]).reshape(T, PEER_HEADS, PEER_TOPK * PEER_TOPK)
    cidx = (i1[..., :, None] * N_KEYS + i2[..., None, :]).reshape(T, PEER_HEADS, PEER_TOPK * PEER_TOPK)
    top, sel = lax.top_k(cand, PEER_TOPK)
    eidx = jnp.take_along_axis(cidx, sel, axis=-1)
    g = jax.nn.softmax(top, axis=-1)
    u = peer_u[eidx]
    act = jax.nn.gelu(jnp.einsum('td,thkd->thk', h, u).astype(jnp.float32), approximate=False)
    w = (g * act).astype(h.dtype)
    return jnp.einsum('thk,thkd->td', w, peer_v[eidx])


def peer_ffn(h, w_q, keys1, keys2, peer_u, peer_v):
    return lax.map(lambda hb: peer_tokens(hb, w_q, keys1, keys2, peer_u, peer_v), h)


def setup_inputs(seed: int = 0) -> dict:
    key = jax.random.key(seed)
    ks = jax.random.split(key, 40)
    f32 = jnp.float32

    def nrm(k, shape, scale):
        return jax.random.normal(k, shape, f32) * scale

    def gain(k, n):
        return 1.0 + 0.02 * jax.random.normal(k, (n,), f32)

    wb = [min(w, PAST_LEN) for w, _ in DILATED_GROUPS]
    return {
        'x_prompt': nrm(ks[0], (BATCH, SEQ, D_MODEL), 1.0),
        'x_sample': nrm(ks[1], (DEC_BATCH, DEC_SEQ, D_MODEL), 1.0),
        'mem_prompt': nrm(ks[2], (BATCH, MEM_TOKENS, D_MODEL), 1.0),
        'cache_k_w128': nrm(ks[3], (DEC_BATCH, wb[0], HEADS_PER_GROUP, HEAD_DIM), 1.0),
        'cache_v_w128': nrm(ks[4], (DEC_BATCH, wb[0], HEADS_PER_GROUP, HEAD_DIM), 1.0),
        'cache_k_w512': nrm(ks[5], (DEC_BATCH, wb[1], HEADS_PER_GROUP, HEAD_DIM), 1.0),
        'cache_v_w512': nrm(ks[6], (DEC_BATCH, wb[1], HEADS_PER_GROUP, HEAD_DIM), 1.0),
        'cache_k_w2048': nrm(ks[7], (DEC_BATCH, wb[2], HEADS_PER_GROUP, HEAD_DIM), 1.0),
        'cache_v_w2048': nrm(ks[8], (DEC_BATCH, wb[2], HEADS_PER_GROUP, HEAD_DIM), 1.0),
        'state_conv': nrm(ks[9], (DEC_BATCH, CONV_WIDTH - 1, CONV_CH), 0.5),
        'cache_mem_k': nrm(ks[10], (DEC_BATCH, MEM_TOKENS, MEM_HEADS, HEAD_DIM), 1.0),
        'cache_mem_v': nrm(ks[11], (DEC_BATCH, MEM_TOKENS, MEM_HEADS, HEAD_DIM), 1.0),
        'norm_mix_w': gain(ks[12], D_MODEL),
        'w_in': nrm(ks[13], (D_MODEL, IN_WIDTH), D_MODEL ** -0.5),
        'q_norm_w': gain(ks[14], HEAD_DIM),
        'k_norm_w': gain(ks[15], HEAD_DIM),
        'conv_w': nrm(ks[16], (CONV_WIDTH, CONV_CH), CONV_WIDTH ** -0.5),
        'conv_b': nrm(ks[17], (CONV_CH,), 0.01),
        'conv_ln_w': gain(ks[18], CONV_CH),
        'conv_ln_b': nrm(ks[19], (CONV_CH,), 0.01),
        'w_out': nrm(ks[20], (ATTN_WIDTH + CONV_CH, D_MODEL), (ATTN_WIDTH + CONV_CH) ** -0.5),
        'norm_mem_w': gain(ks[21], D_MODEL),
        'mem_norm_w': gain(ks[22], D_MODEL),
        'wq_mem': nrm(ks[23], (D_MODEL, MEM_WIDTH), D_MODEL ** -0.5),
        'wk_mem': nrm(ks[24], (D_MODEL, MEM_WIDTH), D_MODEL ** -0.5),
        'wv_mem': nrm(ks[25], (D_MODEL, MEM_WIDTH), D_MODEL ** -0.5),
        'qn_mem_w': gain(ks[26], HEAD_DIM),
        'kn_mem_w': gain(ks[27], HEAD_DIM),
        'wo_mem': nrm(ks[28], (MEM_WIDTH, D_MODEL), MEM_WIDTH ** -0.5),
        'norm_ffn_w': gain(ks[29], D_MODEL),
        'w_peer_q': nrm(ks[30], (D_MODEL, PEER_HEADS * PEER_KEY_DIM), D_MODEL ** -0.5),
        'peer_keys1': nrm(ks[31], (PEER_HEADS, N_KEYS, PEER_KEY_DIM // 2), (PEER_KEY_DIM // 2) ** -0.5),
        'peer_keys2': nrm(ks[32], (PEER_HEADS, N_KEYS, PEER_KEY_DIM // 2), (PEER_KEY_DIM // 2) ** -0.5),
        'peer_u': nrm(ks[33], (N_EXPERTS, D_MODEL), D_MODEL ** -0.5),
        'peer_v': nrm(ks[34], (N_EXPERTS, D_MODEL), (PEER_HEADS * PEER_TOPK) ** -0.5),
    }


def reference(x_prompt, x_sample, mem_prompt, cache_k_w128, cache_v_w128, cache_k_w512, cache_v_w512,
              cache_k_w2048, cache_v_w2048, state_conv, cache_mem_k, cache_mem_v, norm_mix_w, w_in,
              q_norm_w, k_norm_w, conv_w, conv_b, conv_ln_w, conv_ln_b, w_out, norm_mem_w, mem_norm_w,
              wq_mem, wk_mem, wv_mem, qn_mem_w, kn_mem_w, wo_mem, norm_ffn_w, w_peer_q, peer_keys1,
              peer_keys2, peer_u, peer_v):
    S = x_prompt.shape[1]
    T = x_sample.shape[1]
    pos_p = jnp.arange(S, dtype=jnp.int32)
    pos_s = PAST_LEN + jnp.arange(T, dtype=jnp.int32)

    h = rms_norm(x_prompt, norm_mix_w)
    q, k, v, u = mixer_projections(h, w_in, q_norm_w, k_norm_w, pos_p)
    outs, lses, p_win = [], [], []
    for g, (win, dil) in enumerate(DILATED_GROUPS):
        hs = slice(g * HEADS_PER_GROUP, (g + 1) * HEADS_PER_GROUP)
        o, l = dilated_attn_prompt(q[:, :, hs], k[:, :, hs], v[:, :, hs], dil, win // dil)
        outs.append(o)
        lses.append(l)
        keep = min(win, S)
        p_win += [k[:, S - keep:, hs], v[:, S - keep:, hs]]
    u_pad = jnp.pad(u, ((0, 0), (CONV_WIDTH - 1, 0), (0, 0)))
    conv_out = conv_tail(u_pad, conv_w, conv_b, conv_ln_w, conv_ln_b)
    p_conv = u_pad[:, u_pad.shape[1] - (CONV_WIDTH - 1):]
    x1 = x_prompt + jnp.concatenate([combine_groups(outs, lses, h.dtype), conv_out], axis=-1) @ w_out
    p_mem_k, p_mem_v = memory_kv(mem_prompt, mem_norm_w, wk_mem, wv_mem, kn_mem_w)
    x2 = x1 + memory_attend(rms_norm(x1, norm_mem_w), p_mem_k, p_mem_v, wq_mem, qn_mem_w, wo_mem)
    h3 = rms_norm(x2, norm_ffn_w).reshape(-1, PEER_BLOCK, D_MODEL)
    y_prompt = x2 + peer_ffn(h3, w_peer_q, peer_keys1, peer_keys2, peer_u, peer_v).reshape(x2.shape)
    p_k128, p_v128, p_k512, p_v512, p_k2048, p_v2048 = p_win

    h = rms_norm(x_sample, norm_mix_w)
    q, k, v, u = mixer_projections(h, w_in, q_norm_w, k_norm_w, pos_s)
    caches = ((cache_k_w128, cache_v_w128), (cache_k_w512, cache_v_w512), (cache_k_w2048, cache_v_w2048))
    outs, lses, s_win = [], [], []
    for g, ((win, dil), (kc, vc)) in enumerate(zip(DILATED_GROUPS, caches)):
        hs = slice(g * HEADS_PER_GROUP, (g + 1) * HEADS_PER_GROUP)
        o, l, k_all, v_all = dilated_attn_sample(q[:, :, hs], kc, vc, k[:, :, hs], v[:, :, hs], dil, win // dil)
        outs.append(o)
        lses.append(l)
        n_all = k_all.shape[1]
        keep = min(win, n_all)
        s_win += [k_all[:, n_all - keep:], v_all[:, n_all - keep:]]
    u_all = jnp.concatenate([state_conv.astype(u.dtype), u], axis=1)
    conv_out = conv_tail(u_all, conv_w, conv_b, conv_ln_w, conv_ln_b)
    s_conv = u_all[:, u_all.shape[1] - (CONV_WIDTH - 1):]
    x1 = x_sample + jnp.concatenate([combine_groups(outs, lses, h.dtype), conv_out], axis=-1) @ w_out
    x2 = x1 + memory_attend(rms_norm(x1, norm_mem_w), cache_mem_k, cache_mem_v, wq_mem, qn_mem_w, wo_mem)
    h3 = rms_norm(x2, norm_ffn_w)
    y_sample = x2 + peer_ffn(h3, w_peer_q, peer_keys1, peer_keys2, peer_u, peer_v)
    s_k128, s_v128, s_k512, s_v512, s_k2048, s_v2048 = s_win

    return (y_prompt, y_sample, p_k128, p_v128, p_k512, p_v512, p_k2048, p_v2048, p_conv, p_mem_k, p_mem_v,
            s_k128, s_v128, s_k512, s_v512, s_k2048, s_v2048, s_conv)
```

```python
import functools

import numpy as np
import jax
import jax.numpy as jnp
from jax import lax
from jax.experimental import pallas as pl
from jax.experimental.pallas import tpu as pltpu

F32 = jnp.float32
BF16 = jnp.bfloat16

D_MODEL = 2048
HEAD_DIM = 128
HEADS_PER_GROUP = 4
GROUP_W = HEADS_PER_GROUP * HEAD_DIM
DILATIONS = (1, 4, 16)
WINDOWS = (128, 512, 2048)
N_BACK = 128
N_GROUPS = 3
ATTN_W = N_GROUPS * GROUP_W
CONV_CH = 512
CONV_TAPS = 31
QKV_W = 3 * ATTN_W
N_ATTN_HEADS = N_GROUPS * HEADS_PER_GROUP
N_QKV_HEADS = 3 * N_ATTN_HEADS
MEM_TOKENS = 256
MEM_W = 512
PEER_HEADS = 8
N_KEYS = 128
N_EXPERTS = N_KEYS * N_KEYS
TOPK = 16
PAST_LEN = 2048
DEC_SEQ = 4
ROPE_THETA = 10000.0
EPS = 1e-6
SCALE = HEAD_DIM ** -0.5
NEG_INF = float("-inf")

VMEM_LIMIT = 56 * 1024 * 1024


def _cparams(sem):
    return pltpu.CompilerParams(dimension_semantics=sem, vmem_limit_bytes=VMEM_LIMIT)


def _rms(x, w):
    return x * lax.rsqrt(jnp.mean(x * x, axis=-1, keepdims=True) + EPS) * w


def _dot(a, b):
    return jnp.dot(a, b, preferred_element_type=F32)


def _dot_nt(a, b):
    return lax.dot_general(a, b, (((1,), (1,)), ((), ())), preferred_element_type=F32)


def _softmax_pv(s, v):
    m = jnp.max(s, axis=-1, keepdims=True)
    p = jnp.exp(s - m)
    den = jnp.sum(p, axis=-1, keepdims=True)
    return _dot(p.astype(BF16), v) / den, m + jnp.log(den)


def _proj_kernel(x_ref, nw_ref, w_ref, wa_ref, wg_ref, qn_ref, kn_ref, cos_ref, sin_ref,
                 qkv_ref, u_ref, h_scr):
    j = pl.program_id(1)

    @pl.when(j == 0)
    def _():
        h_scr[...] = _rms(x_ref[...], nw_ref[...]).astype(BF16)
        a = _dot(h_scr[...], wa_ref[...])
        g = _dot(h_scr[...], wg_ref[...])
        u_ref[...] = a * jax.nn.sigmoid(g)

    z = _dot(h_scr[...], w_ref[...])

    @pl.when(j < 2 * N_GROUPS)
    def _():
        w = jnp.where(j < N_GROUPS, qn_ref[...], kn_ref[...])
        cos = cos_ref[...]
        sin = sin_ref[...]
        for h in range(HEADS_PER_GROUP):
            y = _rms(z[:, h * HEAD_DIM:(h + 1) * HEAD_DIM], w)
            qkv_ref[h] = y * cos + pltpu.roll(y, HEAD_DIM // 2, 1) * sin

    @pl.when(j >= 2 * N_GROUPS)
    def _():
        for h in range(HEADS_PER_GROUP):
            qkv_ref[h] = z[:, h * HEAD_DIM:(h + 1) * HEAD_DIM]


def _project(x, nw, w_in_bf, qn, kn, cos, sin, tm):
    m = x.shape[0]
    n_tiles = QKV_W // GROUP_W
    return pl.pallas_call(
        _proj_kernel,
        grid=(m // tm, n_tiles),
        in_specs=[
            pl.BlockSpec((tm, D_MODEL), lambda i, j: (i, 0)),
            pl.BlockSpec((1, D_MODEL), lambda i, j: (0, 0)),
            pl.BlockSpec((D_MODEL, GROUP_W), lambda i, j: (0, j)),
            pl.BlockSpec((D_MODEL, GROUP_W), lambda i, j: (0, n_tiles)),
            pl.BlockSpec((D_MODEL, GROUP_W), lambda i, j: (0, n_tiles + 1)),
            pl.BlockSpec((1, HEAD_DIM), lambda i, j: (0, 0)),
            pl.BlockSpec((1, HEAD_DIM), lambda i, j: (0, 0)),
            pl.BlockSpec((tm, HEAD_DIM), lambda i, j: (i, 0)),
            pl.BlockSpec((tm, HEAD_DIM), lambda i, j: (i, 0)),
        ],
        out_specs=[
            pl.BlockSpec((HEADS_PER_GROUP, tm, HEAD_DIM), lambda i, j: (j, i, 0)),
            pl.BlockSpec((tm, GROUP_W), lambda i, j: (i, 0)),
        ],
        out_shape=[
            jax.ShapeDtypeStruct((N_QKV_HEADS, m, HEAD_DIM), F32),
            jax.ShapeDtypeStruct((m, CONV_CH), F32),
        ],
        scratch_shapes=[pltpu.VMEM((tm, D_MODEL), BF16)],
        compiler_params=_cparams(("arbitrary", "arbitrary")),
        name="proj",
    )(x, nw, w_in_bf, w_in_bf, w_in_bf, qn, kn, cos, sin)


def _rope_tables(pos):
    half = HEAD_DIM // 2
    inv = ROPE_THETA ** (-jnp.arange(half, dtype=F32) / half)
    ang = pos.astype(F32)[:, None] * inv[None, :]
    cos = jnp.cos(ang)
    sin = jnp.sin(ang)
    return jnp.concatenate([cos, cos], axis=-1), jnp.concatenate([-sin, sin], axis=-1)


QUERY_BLOCK = 128


def _prompt_attn_kernel(q_ref, k_ref, v_ref, o_ref, lse_ref, kprev, vprev, *, dil):
    i = pl.program_id(0)
    qb = QUERY_BLOCK

    @pl.when(i == 0)
    def _():
        kprev[...] = jnp.zeros_like(kprev)
        vprev[...] = jnp.zeros_like(vprev)

    qi = lax.broadcasted_iota(jnp.int32, (qb, 2 * qb), 0)
    kj = lax.broadcasted_iota(jnp.int32, (qb, 2 * qb), 1)
    valid = (kj >= qi + (qb - N_BACK)) & (kj <= qi + qb) & ((kj >= qb) | (i > 0))
    for h in range(HEADS_PER_GROUP):
        for r in range(dil):
            rows = pl.ds(r, qb, stride=dil) if dil > 1 else pl.ds(0, qb)
            q = q_ref[h, rows, :].astype(BF16)
            k = jnp.concatenate([kprev[h, rows, :], k_ref[h, rows, :]], axis=0).astype(BF16)
            v = jnp.concatenate([vprev[h, rows, :], v_ref[h, rows, :]], axis=0).astype(BF16)
            o, lse = _softmax_pv(jnp.where(valid, _dot_nt(q, k) * SCALE, NEG_INF), v)
            o_ref[h, rows, :] = o
            lse_ref[h, rows, :] = jnp.broadcast_to(lse, (qb, HEAD_DIM))
    kprev[...] = k_ref[...]
    vprev[...] = v_ref[...]


def _prompt_attn(qkv_hm, g):
    s = qkv_hm.shape[1]
    dil = DILATIONS[g]
    sb = QUERY_BLOCK * dil
    blk = (HEADS_PER_GROUP, sb, HEAD_DIM)
    out_spec = pl.BlockSpec(blk, lambda i: (0, i, 0))
    out_shape = jax.ShapeDtypeStruct((HEADS_PER_GROUP, s, HEAD_DIM), F32)
    return pl.pallas_call(
        functools.partial(_prompt_attn_kernel, dil=dil),
        grid=(s // sb,),
        in_specs=[
            pl.BlockSpec(blk, lambda i: (g, i, 0)),
            pl.BlockSpec(blk, lambda i: (N_GROUPS + g, i, 0)),
            pl.BlockSpec(blk, lambda i: (2 * N_GROUPS + g, i, 0)),
        ],
        out_specs=[out_spec, out_spec],
        out_shape=[out_shape, out_shape],
        scratch_shapes=[pltpu.VMEM(blk, F32), pltpu.VMEM(blk, F32)],
        compiler_params=_cparams(("arbitrary",)),
        name=f"prompt_attn_g{g}",
    )(qkv_hm, qkv_hm, qkv_hm)


def _ln_swish(y, lnw, lnb):
    mu = jnp.mean(y, axis=-1, keepdims=True)
    yc = y - mu
    yn = yc * lax.rsqrt(jnp.mean(yc * yc, axis=-1, keepdims=True) + EPS) * lnw + lnb
    return yn * jax.nn.sigmoid(yn)


HALO = 32


def _prompt_conv_kernel(u_ref, halo_ref, cw_ref, cb_ref, lnw_ref, lnb_ref, o_ref, scr):
    i = pl.program_id(0)
    tm = u_ref.shape[0]
    scr[0:HALO, :] = jnp.where(i > 0, halo_ref[...], 0.0)
    scr[HALO:, :] = u_ref[...]
    acc = jnp.zeros((tm, CONV_CH), F32)
    for j in range(CONV_TAPS):
        acc = acc + cw_ref[j:j + 1, :] * scr[pl.ds(j + HALO - (CONV_TAPS - 1), tm), :]
    o_ref[...] = _ln_swish(acc + cb_ref[...], lnw_ref[...], lnb_ref[...])


def _prompt_conv(u, cw, cb, lnw, lnb, tm=512):
    m = u.shape[0]
    vec = pl.BlockSpec((1, CONV_CH), lambda i: (0, 0))
    return pl.pallas_call(
        _prompt_conv_kernel,
        grid=(m // tm,),
        in_specs=[
            pl.BlockSpec((tm, CONV_CH), lambda i: (i, 0)),
            pl.BlockSpec((HALO, CONV_CH), lambda i: (jnp.maximum(i * (tm // HALO) - 1, 0), 0)),
            pl.BlockSpec((CONV_TAPS, CONV_CH), lambda i: (0, 0)),
            vec, vec, vec,
        ],
        out_specs=pl.BlockSpec((tm, CONV_CH), lambda i: (i, 0)),
        out_shape=jax.ShapeDtypeStruct((m, CONV_CH), F32),
        scratch_shapes=[pltpu.VMEM((tm + HALO, CONV_CH), F32)],
        compiler_params=_cparams(("arbitrary",)),
        name="prompt_conv",
    )(u, u, cw, cb, lnw, lnb)


SAMPLE_CONV_ROWS = 40


def _sample_conv_kernel(ucat_ref, wsh_ref, cb_ref, lnw_ref, lnb_ref, o_ref):
    ucat = ucat_ref[...]
    for t in range(DEC_SEQ):
        y = jnp.sum(ucat * wsh_ref[t][None], axis=1)
        o_ref[t] = _ln_swish(y + cb_ref[...], lnw_ref[...], lnb_ref[...])


def _sample_conv(ucat, wsh, cb, lnw, lnb):
    nb = ucat.shape[0]
    return pl.pallas_call(
        _sample_conv_kernel,
        out_shape=jax.ShapeDtypeStruct((DEC_SEQ, nb, CONV_CH), F32),
        compiler_params=pltpu.CompilerParams(vmem_limit_bytes=VMEM_LIMIT),
        name="sample_conv",
    )(ucat, wsh, cb, lnw, lnb)


def _out_proj_kernel(x_ref, o0_ref, o1_ref, o2_ref, l0_ref, l1_ref, l2_ref, cv_ref, w_ref, y_ref, a_scr):
    o_refs = (o0_ref, o1_ref, o2_ref)
    for h in range(HEADS_PER_GROUP):
        l0, l1, l2 = l0_ref[h], l1_ref[h], l2_ref[h]
        lm = jnp.maximum(jnp.maximum(l0, l1), l2)
        es = (jnp.exp(l0 - lm), jnp.exp(l1 - lm), jnp.exp(l2 - lm))
        den = es[0] + es[1] + es[2]
        for g in range(N_GROUPS):
            c0 = g * GROUP_W + h * HEAD_DIM
            a_scr[:, c0:c0 + HEAD_DIM] = ((es[g] / den) * o_refs[g][h]).astype(BF16)
    a_scr[:, ATTN_W:] = cv_ref[...].astype(BF16)
    y_ref[...] = x_ref[...] + _dot(a_scr[...], w_ref[...])


def _out_proj(x, os_, ls_, gidx, conv, w_out_bf, tm):
    m = x.shape[0]
    gspecs = [pl.BlockSpec((HEADS_PER_GROUP, tm, HEAD_DIM), lambda i, gi=gi: (gi, i, 0)) for gi in gidx]
    return pl.pallas_call(
        _out_proj_kernel,
        grid=(m // tm,),
        in_specs=[pl.BlockSpec((tm, D_MODEL), lambda i: (i, 0))] + gspecs + gspecs
        + [pl.BlockSpec((tm, CONV_CH), lambda i: (i, 0)), pl.BlockSpec((D_MODEL, D_MODEL), lambda i: (0, 0))],
        out_specs=pl.BlockSpec((tm, D_MODEL), lambda i: (i, 0)),
        out_shape=jax.ShapeDtypeStruct((m, D_MODEL), F32),
        scratch_shapes=[pltpu.VMEM((tm, D_MODEL), BF16)],
        compiler_params=_cparams(("arbitrary",)),
        name="out_proj",
    )(x, *os_, *ls_, conv, w_out_bf)


def _memory_kv_kernel(mem_ref, nw_ref, wk_ref, wv_ref, kn_ref, k_ref, v_ref):
    h = _rms(mem_ref[...], nw_ref[...]).astype(BF16)
    k = _dot(h, wk_ref[...])
    v_ref[...] = _dot(h, wv_ref[...])
    for hd in range(MEM_W // HEAD_DIM):
        hs = slice(hd * HEAD_DIM, (hd + 1) * HEAD_DIM)
        k_ref[:, hs] = _rms(k[:, hs], kn_ref[...])


def _memory_kv(mem, nw, wk_bf, wv_bf, kn):
    return pl.pallas_call(
        _memory_kv_kernel,
        out_shape=[jax.ShapeDtypeStruct((MEM_TOKENS, MEM_W), F32)] * 2,
        compiler_params=pltpu.CompilerParams(vmem_limit_bytes=VMEM_LIMIT),
        name="memory_kv",
    )(mem, nw, wk_bf, wv_bf, kn)


MEM_HEADS = MEM_W // HEAD_DIM


def _mem_attn_prompt_kernel(x_ref, nw_ref, wq_ref, qn_ref, k_ref, v_ref, wo_ref, y_ref, o_scr):
    x = x_ref[...]
    q = _dot(_rms(x, nw_ref[...]).astype(BF16), wq_ref[...])
    for h in range(MEM_HEADS):
        hs = slice(h * HEAD_DIM, (h + 1) * HEAD_DIM)
        qh = _rms(q[:, hs], qn_ref[...]).astype(BF16)
        s = _dot_nt(qh, k_ref[:, hs].astype(BF16)) * SCALE
        o_scr[:, hs] = _softmax_pv(s, v_ref[:, hs].astype(BF16))[0].astype(BF16)
    y_ref[...] = x + _dot(o_scr[...], wo_ref[...])


def _mem_attn_prompt(x, nw, wq_bf, qn, k, v, wo_bf, tm):
    m = x.shape[0]
    kv_spec = pl.BlockSpec((MEM_TOKENS, MEM_W), lambda i: (0, 0))
    return pl.pallas_call(
        _mem_attn_prompt_kernel,
        grid=(m // tm,),
        in_specs=[
            pl.BlockSpec((tm, D_MODEL), lambda i: (i, 0)),
            pl.BlockSpec((1, D_MODEL), lambda i: (0, 0)),
            pl.BlockSpec((D_MODEL, MEM_W), lambda i: (0, 0)),
            pl.BlockSpec((1, HEAD_DIM), lambda i: (0, 0)),
            kv_spec, kv_spec,
            pl.BlockSpec((MEM_W, D_MODEL), lambda i: (0, 0)),
        ],
        out_specs=pl.BlockSpec((tm, D_MODEL), lambda i: (i, 0)),
        out_shape=jax.ShapeDtypeStruct((m, D_MODEL), F32),
        scratch_shapes=[pltpu.VMEM((tm, MEM_W), BF16)],
        compiler_params=_cparams(("arbitrary",)),
        name="mem_attn_prompt",
    )(x, nw, wq_bf, qn, k, v, wo_bf)


MEM_BB = 8
MEM_ROWS = MEM_BB * DEC_SEQ
MEM_KEYS = MEM_BB * MEM_TOKENS * MEM_HEADS


def _mem_attn_sample_kernel(x_ref, nw_ref, wq_ref, qn_ref, k_ref, v_ref, bias_ref, wo_ref, y_ref, o_scr):
    x = x_ref[...]
    q = _dot(_rms(x, nw_ref[...]).astype(BF16), wq_ref[...])
    qs = jnp.concatenate([_rms(q[:, h * HEAD_DIM:(h + 1) * HEAD_DIM], qn_ref[...]) for h in range(MEM_HEADS)],
                         axis=0).astype(BF16)
    s = _dot_nt(qs, k_ref[...].astype(BF16)) * SCALE + bias_ref[...]
    o = _softmax_pv(s, v_ref[...].astype(BF16))[0]
    for h in range(MEM_HEADS):
        o_scr[:, h * HEAD_DIM:(h + 1) * HEAD_DIM] = o[h * MEM_ROWS:(h + 1) * MEM_ROWS].astype(BF16)
    y_ref[...] = x + _dot(o_scr[...], wo_ref[...])


def _mem_sample_bias():
    r = np.arange(MEM_HEADS * MEM_ROWS)[:, None]
    c = np.arange(MEM_KEYS)[None, :]
    ok = (r // MEM_ROWS == c % MEM_HEADS) & ((r % MEM_ROWS) // DEC_SEQ == c // (MEM_TOKENS * MEM_HEADS))
    return np.where(ok, 0.0, NEG_INF).astype(np.float32)


def _mem_attn_sample(x, nw, wq_bf, qn, k_flat, v_flat, wo_bf):
    m = x.shape[0]
    bias = jnp.asarray(_mem_sample_bias())
    kv_spec = pl.BlockSpec((MEM_KEYS, HEAD_DIM), lambda i: (i, 0))
    return pl.pallas_call(
        _mem_attn_sample_kernel,
        grid=(m // MEM_ROWS,),
        in_specs=[
            pl.BlockSpec((MEM_ROWS, D_MODEL), lambda i: (i, 0)),
            pl.BlockSpec((1, D_MODEL), lambda i: (0, 0)),
            pl.BlockSpec((D_MODEL, MEM_W), lambda i: (0, 0)),
            pl.BlockSpec((1, HEAD_DIM), lambda i: (0, 0)),
            kv_spec, kv_spec,
            pl.BlockSpec(bias.shape, lambda i: (0, 0)),
            pl.BlockSpec((MEM_W, D_MODEL), lambda i: (0, 0)),
        ],
        out_specs=pl.BlockSpec((MEM_ROWS, D_MODEL), lambda i: (i, 0)),
        out_shape=jax.ShapeDtypeStruct((m, D_MODEL), F32),
        scratch_shapes=[pltpu.VMEM((MEM_ROWS, MEM_W), BF16)],
        compiler_params=_cparams(("arbitrary",)),
        name="mem_attn_sample",
    )(x, nw, wq_bf, qn, k_flat, v_flat, bias, wo_bf)


PEER_TQ = 128
NOT_TOP = 99.0
CAND_ROWS = 16 + 7 * 8 + 8
INVALID_FLAT = 999.0


def _cand_flat_index():
    flat = np.full((CAND_ROWS,), INVALID_FLAT, np.float32)
    flat[0:16] = np.arange(16)
    for a in range(1, 8):
        nb = TOPK // (a + 1)
        flat[16 + 8 * (a - 1):16 + 8 * (a - 1) + nb] = a * TOPK + np.arange(nb)
    flat[72:80] = np.arange(8, 16) * TOPK
    return np.broadcast_to(flat[:, None], (CAND_ROWS, PEER_TQ)).copy()


def _topk_rows(s, row, vals_scr):
    rank = jnp.full(s.shape, NOT_TOP, F32)
    for a in range(TOPK):
        m = jnp.max(s, axis=0, keepdims=True)
        idx = jnp.min(jnp.where(s == m, row, float(N_KEYS)), axis=0, keepdims=True)
        hit = row == idx
        rank = jnp.where(hit, float(a), rank)
        s = jnp.where(hit, NEG_INF, s)
        vals_scr[a:a + 1, :] = m
    return rank


def _peer_query_kernel(x_ref, nw_ref, wq_ref, k1_ref, k2_ref, flat_ref,
                       h3t_ref, r2_ref, e2_ref, n1_ref, c1_ref, q_scr, v1_scr, v2_scr):
    h3 = _rms(x_ref[...], nw_ref[...])
    h3t_ref[...] = h3.T.astype(BF16)
    q_scr[...] = _dot(h3.astype(BF16), wq_ref[...]).astype(BF16)
    row = lax.broadcasted_iota(jnp.int32, (N_KEYS, PEER_TQ), 0).astype(F32)
    flat = flat_ref[...]
    cand_ok = flat < INVALID_FLAT

    def head(h, carry):
        off = pl.multiple_of(h * 2 * HEAD_DIM, 2 * HEAD_DIM)
        s1 = _dot_nt(k1_ref[h], q_scr[:, pl.ds(off, HEAD_DIM)])
        s2 = _dot_nt(k2_ref[h], q_scr[:, pl.ds(off + HEAD_DIM, HEAD_DIM)])
        rank1 = _topk_rows(s1, row, v1_scr)
        rank2 = _topk_rows(s2, row, v2_scr)
        pieces = [v1_scr[0:1, :] + v2_scr[...]]
        for a in range(1, 8):
            pieces.append(v1_scr[a:a + 1, :] + v2_scr[0:8, :])
        pieces.append(v1_scr[8:16, :] + v2_scr[0:1, :])
        cand0 = jnp.where(cand_ok, jnp.concatenate(pieces, axis=0), NEG_INF)
        cand = cand0
        for _ in range(TOPK):
            m = jnp.max(cand, axis=0, keepdims=True)
            f = jnp.min(jnp.where(cand == m, flat, 2 * INVALID_FLAT), axis=0, keepdims=True)
            cand = jnp.where(flat == f, NEG_INF, cand)
        sel = cand_ok & (cand == NEG_INF)
        z = jnp.sum(jnp.where(sel, jnp.exp(cand0 - cand0[0:1, :]), 0.0), axis=0, keepdims=True)
        self_ = sel.astype(F32)
        n1 = jnp.zeros((N_KEYS, PEER_TQ), F32)
        for a in range(TOPK):
            if a == 0:
                na = jnp.sum(self_[0:16, :], axis=0, keepdims=True)
            elif a < 8:
                na = jnp.sum(self_[16 + 8 * (a - 1):16 + 8 * a, :], axis=0, keepdims=True)
            else:
                na = self_[72 + a - 8:72 + a - 7, :]
            n1 = jnp.where(rank1 == float(a), na, n1)
        r2_ref[h] = rank2
        e2_ref[h] = jnp.exp(s2 - v2_scr[0:1, :])
        n1_ref[h] = n1
        c1_ref[h] = jnp.exp(s1 - v1_scr[0:1, :]) / z
        return carry

    lax.fori_loop(0, PEER_HEADS, head, 0)


def _peer_query(x2, nw, wq_bf, k1_bf, k2_bf):
    m = x2.shape[0]
    tq = PEER_TQ
    flat = jnp.asarray(_cand_flat_index())
    key_spec = pl.BlockSpec((PEER_HEADS, N_KEYS, HEAD_DIM), lambda i: (0, 0, 0))
    stat_spec = pl.BlockSpec((PEER_HEADS, N_KEYS, tq), lambda i: (0, 0, i))
    stat_shape = jax.ShapeDtypeStruct((PEER_HEADS, N_KEYS, m), F32)
    return pl.pallas_call(
        _peer_query_kernel,
        grid=(m // tq,),
        in_specs=[
            pl.BlockSpec((tq, D_MODEL), lambda i: (i, 0)),
            pl.BlockSpec((1, D_MODEL), lambda i: (0, 0)),
            pl.BlockSpec((D_MODEL, 2 * HEAD_DIM * PEER_HEADS), lambda i: (0, 0)),
            key_spec, key_spec,
            pl.BlockSpec((CAND_ROWS, tq), lambda i: (0, 0)),
        ],
        out_specs=[pl.BlockSpec((D_MODEL, tq), lambda i: (0, i))] + [stat_spec] * 4,
        out_shape=[jax.ShapeDtypeStruct((D_MODEL, m), BF16)] + [stat_shape] * 4,
        scratch_shapes=[
            pltpu.VMEM((tq, 2 * HEAD_DIM * PEER_HEADS), BF16),
            pltpu.VMEM((TOPK, tq), F32),
            pltpu.VMEM((TOPK, tq), F32),
        ],
        compiler_params=_cparams(("arbitrary",)),
        name="peer_query",
    )(x2, nw, wq_bf, k1_bf, k2_bf, flat)


PEER_TM = 512
PEER_TE = 1024
LANE = 128
INV_SQRT2 = 0.7071067811865476


def _peer_dense_kernel(h3t_ref, u_ref, vt_ref, r2_ref, e2_ref, n1_ref, c1_ref, x2_ref, y_ref,
                       acc_scr, act_scr, wt_scr):
    j = pl.program_id(1)

    @pl.when(j == 0)
    def _():
        acc_scr[...] = jnp.zeros_like(acc_scr)

    act_scr[...] = _dot(u_ref[...], h3t_ref[...])
    for c in range(PEER_TE // N_KEYS):
        for lc in range(PEER_TM // LANE):
            ls = slice(lc * LANE, (lc + 1) * LANE)
            gate = jnp.zeros((N_KEYS, LANE), F32)
            for h in range(PEER_HEADS):
                n1 = n1_ref[h, c:c + 1, ls]
                c1 = c1_ref[h, c:c + 1, ls]
                gate = gate + jnp.where(r2_ref[h, :, ls] < n1, e2_ref[h, :, ls], 0.0) * c1
            a = act_scr[c * N_KEYS:(c + 1) * N_KEYS, ls]
            gelu = 0.5 * a * (1.0 + lax.erf(a * INV_SQRT2))
            wt_scr[c * N_KEYS:(c + 1) * N_KEYS, ls] = (gate * gelu).astype(BF16)
    acc_scr[...] += _dot(vt_ref[...], wt_scr[...])

    @pl.when(j == pl.num_programs(1) - 1)
    def _():
        y_ref[...] = x2_ref[...] + acc_scr[...].T


def _peer_dense(h3t, u_bf, vt_bf, r2, e2, n1, c1, x2):
    m = x2.shape[0]
    tm, te = PEER_TM, PEER_TE
    once = pl.Buffered(1)
    key2_spec = pl.BlockSpec((PEER_HEADS, N_KEYS, tm), lambda i, j: (0, 0, i), pipeline_mode=once)
    key1_spec = pl.BlockSpec((PEER_HEADS, te // N_KEYS, tm), lambda i, j: (0, j, i))
    return pl.pallas_call(
        _peer_dense_kernel,
        grid=(m // tm, N_EXPERTS // te),
        in_specs=[
            pl.BlockSpec((D_MODEL, tm), lambda i, j: (0, i)),
            pl.BlockSpec((te, D_MODEL), lambda i, j: (j, 0)),
            pl.BlockSpec((D_MODEL, te), lambda i, j: (0, j)),
            key2_spec, key2_spec, key1_spec, key1_spec,
            pl.BlockSpec((tm, D_MODEL), lambda i, j: (i, 0), pipeline_mode=once),
        ],
        out_specs=pl.BlockSpec((tm, D_MODEL), lambda i, j: (i, 0)),
        out_shape=jax.ShapeDtypeStruct((m, D_MODEL), F32),
        scratch_shapes=[
            pltpu.VMEM((D_MODEL, tm), F32),
            pltpu.VMEM((te, tm), F32),
            pltpu.VMEM((te, tm), BF16),
        ],
        compiler_params=_cparams(("arbitrary", "arbitrary")),
        name="peer_dense",
    )(h3t, u_bf, vt_bf, r2, e2, n1, c1, x2)


def _peer(x2, nw, wq_bf, k1_bf, k2_bf, u_bf, vt_bf):
    h3t, r2, e2, n1, c1 = _peer_query(x2, nw, wq_bf, k1_bf, k2_bf)
    return _peer_dense(h3t, u_bf, vt_bf, r2, e2, n1, c1, x2)


SAMPLE_BB = 2
SAMPLE_ROWS = SAMPLE_BB * DEC_SEQ


SAMPLE_Q = HEADS_PER_GROUP * SAMPLE_ROWS
NEW_PAD = 128
SAMPLE_NKC = tuple(SAMPLE_BB * HEADS_PER_GROUP * n for n in (WINDOWS[0], WINDOWS[1], DEC_SEQ * N_BACK))


def _sample_bias(g):
    nkc = SAMPLE_NKC[g]
    r = np.arange(SAMPLE_Q)[:, None]
    rh, rb, rt = r // SAMPLE_ROWS, (r % SAMPLE_ROWS) // DEC_SEQ, r % DEC_SEQ
    c = np.arange(nkc)[None, :]
    per_batch = nkc // SAMPLE_BB
    cb, ch = c // per_batch, c % HEADS_PER_GROUP
    if g == 0:
        ok_pos = (c % per_batch) // HEADS_PER_GROUP >= rt
    elif g == 1:
        ok_pos = ((c % per_batch) // HEADS_PER_GROUP) % DILATIONS[1] == rt
    else:
        ok_pos = (c % (DEC_SEQ * HEADS_PER_GROUP)) // HEADS_PER_GROUP == rt
    ok_c = (cb == rb) & (ch == rh) & ok_pos
    n = np.arange(NEW_PAD)[None, :]
    nh, nb, nt = n // SAMPLE_ROWS, (n % SAMPLE_ROWS) // DEC_SEQ, n % DEC_SEQ
    ok_n = (n < SAMPLE_Q) & (nh == rh) & (nb == rb) & ((nt <= rt) if g == 0 else (nt == rt))
    return np.where(np.concatenate([ok_c, ok_n], axis=1), 0.0, NEG_INF).astype(np.float32)


def _sample_attn_kernel(qkv_ref, k0_ref, v0_ref, k1_ref, v1_ref, k2_ref, v2_ref, b0_ref, b1_ref, b2_ref,
                        o_ref, lse_ref):
    caches = ((k0_ref, v0_ref), (k1_ref, v1_ref), (k2_ref, v2_ref))
    biases = (b0_ref, b1_ref, b2_ref)
    pad = jnp.zeros((NEW_PAD - SAMPLE_Q, HEAD_DIM), F32)

    def stack(first_head):
        return [qkv_ref[first_head + h] for h in range(HEADS_PER_GROUP)]

    for g in range(N_GROUPS):
        h0 = g * HEADS_PER_GROUP
        q = jnp.concatenate(stack(h0), axis=0).astype(BF16)
        kc = caches[g][0][...].reshape(SAMPLE_NKC[g], HEAD_DIM)
        vc = caches[g][1][...].reshape(SAMPLE_NKC[g], HEAD_DIM)
        k = jnp.concatenate([kc] + stack(N_ATTN_HEADS + h0) + [pad], axis=0).astype(BF16)
        v = jnp.concatenate([vc] + stack(2 * N_ATTN_HEADS + h0) + [pad], axis=0).astype(BF16)
        s = _dot_nt(q, k) * SCALE + biases[g][...]
        o, lse = _softmax_pv(s, v)
        for h in range(HEADS_PER_GROUP):
            rows = slice(h * SAMPLE_ROWS, (h + 1) * SAMPLE_ROWS)
            o_ref[h0 + h] = o[rows]
            lse_ref[h0 + h] = jnp.broadcast_to(lse[rows], (SAMPLE_ROWS, HEAD_DIM))


def _sample_attn(qkv_hm, caches):
    m = qkv_hm.shape[1]
    nbatch = m // DEC_SEQ
    bb = SAMPLE_BB
    args = [qkv_hm]
    specs = [pl.BlockSpec((N_QKV_HEADS, SAMPLE_ROWS, HEAD_DIM), lambda i: (0, i, 0))]
    for g in range(N_GROUPS):
        for cache in caches[2 * g:2 * g + 2]:
            if g < 2:
                args.append(cache.reshape(-1, HEAD_DIM))
                specs.append(pl.BlockSpec((SAMPLE_NKC[g], HEAD_DIM), lambda i: (i, 0)))
            else:
                lanes = DILATIONS[2] * HEADS_PER_GROUP
                args.append(cache.reshape(-1, lanes, HEAD_DIM))
                specs.append(pl.BlockSpec((bb * N_BACK, DEC_SEQ * HEADS_PER_GROUP, HEAD_DIM), lambda i: (i, 0, 0)))
    for g in range(N_GROUPS):
        bias = jnp.asarray(_sample_bias(g))
        args.append(bias)
        specs.append(pl.BlockSpec(bias.shape, lambda i: (0, 0)))
    out_spec = pl.BlockSpec((N_ATTN_HEADS, SAMPLE_ROWS, HEAD_DIM), lambda i: (0, i, 0))
    out_shape = jax.ShapeDtypeStruct((N_ATTN_HEADS, m, HEAD_DIM), F32)
    return pl.pallas_call(
        _sample_attn_kernel,
        grid=(nbatch // bb,),
        in_specs=specs,
        out_specs=[out_spec, out_spec],
        out_shape=[out_shape, out_shape],
        compiler_params=_cparams(("arbitrary",)),
        name="sample_attn",
    )(*args)


NEW_ROWS = DEC_SEQ * HEADS_PER_GROUP


def _cache_shift_kernel(*refs):
    n = 2 * N_GROUPS
    old, new, out, sem = refs[:n], refs[n:2 * n], refs[2 * n:3 * n], refs[3 * n]
    copies = []
    for c in range(n):
        keep = old[c].shape[1] - NEW_ROWS
        copies.append(pltpu.make_async_copy(
            old[c].at[:, pl.ds(NEW_ROWS, keep), :], out[c].at[:, pl.ds(0, keep), :], sem.at[2 * c]))
        copies.append(pltpu.make_async_copy(new[c], out[c].at[:, pl.ds(keep, NEW_ROWS), :], sem.at[2 * c + 1]))
    for cp in copies:
        cp.start()
    for cp in copies:
        cp.wait()


def _cache_shift(caches, new_rows):
    n = 2 * N_GROUPS
    nbatch = caches[0].shape[0]
    flat = [c.reshape(nbatch, c.shape[1] * HEADS_PER_GROUP, HEAD_DIM) for c in caches]
    any_spec = pl.BlockSpec(memory_space=pl.ANY)
    outs = pl.pallas_call(
        _cache_shift_kernel,
        in_specs=[any_spec] * (2 * n),
        out_specs=[any_spec] * n,
        out_shape=[jax.ShapeDtypeStruct(c.shape, F32) for c in flat],
        scratch_shapes=[pltpu.SemaphoreType.DMA((2 * n,))],
        name="cache_shift",
    )(*flat, *new_rows)
    return [o.reshape(c.shape) for o, c in zip(outs, caches)]


def kernel(x_prompt, x_sample, mem_prompt, cache_k_w128, cache_v_w128, cache_k_w512, cache_v_w512, cache_k_w2048, cache_v_w2048, state_conv, cache_mem_k, cache_mem_v, norm_mix_w, w_in, q_norm_w, k_norm_w, conv_w, conv_b, conv_ln_w, conv_ln_b, w_out, norm_mem_w, mem_norm_w, wq_mem, wk_mem, wv_mem, qn_mem_w, kn_mem_w, wo_mem, norm_ffn_w, w_peer_q, peer_keys1, peer_keys2, peer_u, peer_v):
    s = x_prompt.shape[1]
    nbatch, t_new = x_sample.shape[:2]
    ms = nbatch * t_new

    row = lambda w: w.reshape(1, -1)
    w_in_bf = w_in.astype(BF16)
    w_out_bf = w_out.astype(BF16)
    wq_mem_bf, wk_mem_bf, wv_mem_bf, wo_mem_bf = (w.astype(BF16) for w in (wq_mem, wk_mem, wv_mem, wo_mem))
    wq_peer_bf = w_peer_q.astype(BF16)
    k1_bf = peer_keys1.astype(BF16)
    k2_bf = peer_keys2.astype(BF16)
    u_bf = peer_u.astype(BF16)
    vt_bf = peer_v.astype(BF16).T
    peer_w = (row(norm_ffn_w), wq_peer_bf, k1_bf, k2_bf, u_bf, vt_bf)
    conv_vecs = (row(conv_b), row(conv_ln_w), row(conv_ln_b))

    xp = x_prompt.reshape(s, D_MODEL)
    cos_p, sin_p = _rope_tables(jnp.arange(s, dtype=jnp.int32))
    qkv_p, u_p = _project(xp, row(norm_mix_w), w_in_bf, row(q_norm_w), row(k_norm_w), cos_p, sin_p, tm=512)
    attn = [_prompt_attn(qkv_p, g) for g in range(N_GROUPS)]
    conv_p = _prompt_conv(u_p, conv_w, *conv_vecs)
    x1_p = _out_proj(xp, [a[0] for a in attn], [a[1] for a in attn], (0, 0, 0), conv_p, w_out_bf, tm=512)
    mem_k, mem_v = _memory_kv(mem_prompt.reshape(MEM_TOKENS, D_MODEL), row(mem_norm_w), wk_mem_bf, wv_mem_bf,
                              row(kn_mem_w))
    x2_p = _mem_attn_prompt(x1_p, row(norm_mem_w), wq_mem_bf, row(qn_mem_w), mem_k, mem_v, wo_mem_bf, tm=512)
    y_p = _peer(x2_p, *peer_w)

    def group_heads(qkv_hm, which, g):
        h0 = which * N_ATTN_HEADS + g * HEADS_PER_GROUP
        return qkv_hm[h0:h0 + HEADS_PER_GROUP]

    p_win = []
    for g, win in enumerate(WINDOWS):
        keep = min(win, s)
        for which in (1, 2):
            heads = group_heads(qkv_p, which, g)[:, s - keep:]
            p_win.append(heads.transpose(1, 0, 2)[None])
    p_conv = u_p[s - (CONV_TAPS - 1):].reshape(1, CONV_TAPS - 1, CONV_CH)
    kv4 = lambda a: a.reshape(1, MEM_TOKENS, MEM_HEADS, HEAD_DIM)

    xs = x_sample.reshape(ms, D_MODEL)
    cos_s, sin_s = _rope_tables(PAST_LEN + jnp.arange(ms, dtype=jnp.int32) % t_new)
    qkv_s, u_s = _project(xs, row(norm_mix_w), w_in_bf, row(q_norm_w), row(k_norm_w), cos_s, sin_s, tm=ms)
    caches = (cache_k_w128, cache_v_w128, cache_k_w512, cache_v_w512, cache_k_w2048, cache_v_w2048)
    o_s, lse_s = _sample_attn(qkv_s, caches)
    new_rows = []
    for g in range(N_GROUPS):
        for which in (1, 2):
            heads = group_heads(qkv_s, which, g).reshape(HEADS_PER_GROUP, nbatch, t_new, HEAD_DIM)
            new_rows.append(heads.transpose(1, 2, 0, 3).reshape(nbatch, NEW_ROWS, HEAD_DIM))
    s_win = _cache_shift(caches, new_rows)

    n_state = CONV_TAPS - 1
    ucat = jnp.concatenate(
        [state_conv, u_s.reshape(nbatch, t_new, CONV_CH),
         jnp.zeros((nbatch, SAMPLE_CONV_ROWS - n_state - t_new, CONV_CH), F32)], axis=1)
    wsh = jnp.stack([jnp.pad(conv_w, ((t, SAMPLE_CONV_ROWS - CONV_TAPS - t), (0, 0))) for t in range(t_new)])
    conv_s = _sample_conv(ucat, wsh, *conv_vecs)
    conv_s = conv_s.transpose(1, 0, 2).reshape(ms, CONV_CH)
    s_conv = ucat[:, t_new:t_new + n_state]
    x1_s = _out_proj(xs, [o_s] * N_GROUPS, [lse_s] * N_GROUPS, (0, 1, 2), conv_s, w_out_bf, tm=ms)
    x2_s = _mem_attn_sample(x1_s, row(norm_mem_w), wq_mem_bf, row(qn_mem_w),
                            cache_mem_k.reshape(-1, HEAD_DIM), cache_mem_v.reshape(-1, HEAD_DIM), wo_mem_bf)
    y_s = _peer(x2_s, *peer_w)

    return (y_p.reshape(x_prompt.shape), y_s.reshape(x_sample.shape), *p_win, p_conv, kv4(mem_k), kv4(mem_v),
            *s_win, s_conv)
```

```python
import functools

import numpy as np
import jax
import jax.numpy as jnp
from jax import lax
from jax.experimental import pallas as pl
from jax.experimental.pallas import tpu as pltpu

F32 = jnp.float32
BF16 = jnp.bfloat16

D_MODEL = 2048
HEAD_DIM = 128
HEADS_PER_GROUP = 4
GROUP_W = HEADS_PER_GROUP * HEAD_DIM
DILATIONS = (1, 4, 16)
WINDOWS = (128, 512, 2048)
N_BACK = 128
N_GROUPS = 3
ATTN_W = N_GROUPS * GROUP_W
CONV_CH = 512
CONV_TAPS = 31
QKV_W = 3 * ATTN_W
N_ATTN_HEADS = N_GROUPS * HEADS_PER_GROUP
N_QKV_HEADS = 3 * N_ATTN_HEADS
MEM_TOKENS = 256
MEM_W = 512
PEER_HEADS = 8
N_KEYS = 128
N_EXPERTS = N_KEYS * N_KEYS
TOPK = 16
PAST_LEN = 2048
DEC_SEQ = 4
ROPE_THETA = 10000.0
EPS = 1e-6
SCALE = HEAD_DIM ** -0.5
NEG_INF = float("-inf")

VMEM_LIMIT = 56 * 1024 * 1024


def _cparams(sem):
    return pltpu.CompilerParams(dimension_semantics=sem, vmem_limit_bytes=VMEM_LIMIT)


def _rms(x, w):
    return x * lax.rsqrt(jnp.mean(x * x, axis=-1, keepdims=True) + EPS) * w


def _dot(a, b):
    return jnp.dot(a, b, preferred_element_type=F32)


def _dot_nt(a, b):
    return lax.dot_general(a, b, (((1,), (1,)), ((), ())), preferred_element_type=F32)


def _softmax_pv(s, v):
    m = jnp.max(s, axis=-1, keepdims=True)
    p = jnp.exp(s - m)
    den = jnp.sum(p, axis=-1, keepdims=True)
    return _dot(p.astype(BF16), v) / den, m + jnp.log(den)


def _proj_kernel(x_ref, nw_ref, w_ref, wa_ref, wg_ref, qn_ref, kn_ref, cos_ref, sin_ref,
                 qkv_ref, u_ref, h_scr):
    j = pl.program_id(1)

    @pl.when(j == 0)
    def _():
        h_scr[...] = _rms(x_ref[...], nw_ref[...]).astype(BF16)
        a = _dot(h_scr[...], wa_ref[...])
        g = _dot(h_scr[...], wg_ref[...])
        u_ref[...] = a * jax.nn.sigmoid(g)

    z = _dot(h_scr[...], w_ref[...])

    @pl.when(j < 2 * N_GROUPS)
    def _():
        w = jnp.where(j < N_GROUPS, qn_ref[...], kn_ref[...])
        cos = cos_ref[...]
        sin = sin_ref[...]
        for h in range(HEADS_PER_GROUP):
            y = _rms(z[:, h * HEAD_DIM:(h + 1) * HEAD_DIM], w)
            qkv_ref[h] = y * cos + pltpu.roll(y, HEAD_DIM // 2, 1) * sin

    @pl.when(j >= 2 * N_GROUPS)
    def _():
        for h in range(HEADS_PER_GROUP):
            qkv_ref[h] = z[:, h * HEAD_DIM:(h + 1) * HEAD_DIM]


def _project(x, nw, w_in_bf, qn, kn, cos, sin, tm):
    m = x.shape[0]
    n_tiles = QKV_W // GROUP_W
    return pl.pallas_call(
        _proj_kernel,
        grid=(m // tm, n_tiles),
        in_specs=[
            pl.BlockSpec((tm, D_MODEL), lambda i, j: (i, 0)),
            pl.BlockSpec((1, D_MODEL), lambda i, j: (0, 0)),
            pl.BlockSpec((D_MODEL, GROUP_W), lambda i, j: (0, j)),
            pl.BlockSpec((D_MODEL, GROUP_W), lambda i, j: (0, n_tiles)),
            pl.BlockSpec((D_MODEL, GROUP_W), lambda i, j: (0, n_tiles + 1)),
            pl.BlockSpec((1, HEAD_DIM), lambda i, j: (0, 0)),
            pl.BlockSpec((1, HEAD_DIM), lambda i, j: (0, 0)),
            pl.BlockSpec((tm, HEAD_DIM), lambda i, j: (i, 0)),
            pl.BlockSpec((tm, HEAD_DIM), lambda i, j: (i, 0)),
        ],
        out_specs=[
            pl.BlockSpec((HEADS_PER_GROUP, tm, HEAD_DIM), lambda i, j: (j, i, 0)),
            pl.BlockSpec((tm, GROUP_W), lambda i, j: (i, 0)),
        ],
        out_shape=[
            jax.ShapeDtypeStruct((N_QKV_HEADS, m, HEAD_DIM), F32),
            jax.ShapeDtypeStruct((m, CONV_CH), F32),
        ],
        scratch_shapes=[pltpu.VMEM((tm, D_MODEL), BF16)],
        compiler_params=_cparams(("arbitrary", "arbitrary")),
        name="proj",
    )(x, nw, w_in_bf, w_in_bf, w_in_bf, qn, kn, cos, sin)


def _rope_tables(pos):
    half = HEAD_DIM // 2
    inv = ROPE_THETA ** (-jnp.arange(half, dtype=F32) / half)
    ang = pos.astype(F32)[:, None] * inv[None, :]
    cos = jnp.cos(ang)
    sin = jnp.sin(ang)
    return jnp.concatenate([cos, cos], axis=-1), jnp.concatenate([-sin, sin], axis=-1)


QUERY_BLOCK = 128


def _prompt_attn_kernel(q_ref, k_ref, v_ref, o_ref, lse_ref, kprev, vprev, *, dil):
    i = pl.program_id(0)
    qb = QUERY_BLOCK

    @pl.when(i == 0)
    def _():
        kprev[...] = jnp.zeros_like(kprev)
        vprev[...] = jnp.zeros_like(vprev)

    qi = lax.broadcasted_iota(jnp.int32, (qb, 2 * qb), 0)
    kj = lax.broadcasted_iota(jnp.int32, (qb, 2 * qb), 1)
    valid = (kj >= qi + (qb - N_BACK)) & (kj <= qi + qb) & ((kj >= qb) | (i > 0))
    for h in range(HEADS_PER_GROUP):
        for r in range(dil):
            rows = pl.ds(r, qb, stride=dil) if dil > 1 else pl.ds(0, qb)
            q = q_ref[h, rows, :].astype(BF16)
            k = jnp.concatenate([kprev[h, rows, :], k_ref[h, rows, :]], axis=0).astype(BF16)
            v = jnp.concatenate([vprev[h, rows, :], v_ref[h, rows, :]], axis=0).astype(BF16)
            o, lse = _softmax_pv(jnp.where(valid, _dot_nt(q, k) * SCALE, NEG_INF), v)
            o_ref[h, rows, :] = o
            lse_ref[h, rows, :] = jnp.broadcast_to(lse, (qb, HEAD_DIM))
    kprev[...] = k_ref[...]
    vprev[...] = v_ref[...]


def _prompt_attn(qkv_hm, g):
    s = qkv_hm.shape[1]
    dil = DILATIONS[g]
    sb = QUERY_BLOCK * dil
    blk = (HEADS_PER_GROUP, sb, HEAD_DIM)
    out_spec = pl.BlockSpec(blk, lambda i: (0, i, 0))
    out_shape = jax.ShapeDtypeStruct((HEADS_PER_GROUP, s, HEAD_DIM), F32)
    return pl.pallas_call(
        functools.partial(_prompt_attn_kernel, dil=dil),
        grid=(s // sb,),
        in_specs=[
            pl.BlockSpec(blk, lambda i: (g, i, 0)),
            pl.BlockSpec(blk, lambda i: (N_GROUPS + g, i, 0)),
            pl.BlockSpec(blk, lambda i: (2 * N_GROUPS + g, i, 0)),
        ],
        out_specs=[out_spec, out_spec],
        out_shape=[out_shape, out_shape],
        scratch_shapes=[pltpu.VMEM(blk, F32), pltpu.VMEM(blk, F32)],
        compiler_params=_cparams(("arbitrary",)),
        name=f"prompt_attn_g{g}",
    )(qkv_hm, qkv_hm, qkv_hm)


def _ln_swish(y, lnw, lnb):
    mu = jnp.mean(y, axis=-1, keepdims=True)
    yc = y - mu
    yn = yc * lax.rsqrt(jnp.mean(yc * yc, axis=-1, keepdims=True) + EPS) * lnw + lnb
    return yn * jax.nn.sigmoid(yn)


HALO = 32


def _prompt_conv_kernel(u_ref, halo_ref, cw_ref, cb_ref, lnw_ref, lnb_ref, o_ref, scr):
    i = pl.program_id(0)
    tm = u_ref.shape[0]
    scr[0:HALO, :] = jnp.where(i > 0, halo_ref[...], 0.0)
    scr[HALO:, :] = u_ref[...]
    acc = jnp.zeros((tm, CONV_CH), F32)
    for j in range(CONV_TAPS):
        acc = acc + cw_ref[j:j + 1, :] * scr[pl.ds(j + HALO - (CONV_TAPS - 1), tm), :]
    o_ref[...] = _ln_swish(acc + cb_ref[...], lnw_ref[...], lnb_ref[...])


def _prompt_conv(u, cw, cb, lnw, lnb, tm=512):
    m = u.shape[0]
    vec = pl.BlockSpec((1, CONV_CH), lambda i: (0, 0))
    return pl.pallas_call(
        _prompt_conv_kernel,
        grid=(m // tm,),
        in_specs=[
            pl.BlockSpec((tm, CONV_CH), lambda i: (i, 0)),
            pl.BlockSpec((HALO, CONV_CH), lambda i: (jnp.maximum(i * (tm // HALO) - 1, 0), 0)),
            pl.BlockSpec((CONV_TAPS, CONV_CH), lambda i: (0, 0)),
            vec, vec, vec,
        ],
        out_specs=pl.BlockSpec((tm, CONV_CH), lambda i: (i, 0)),
        out_shape=jax.ShapeDtypeStruct((m, CONV_CH), F32),
        scratch_shapes=[pltpu.VMEM((tm + HALO, CONV_CH), F32)],
        compiler_params=_cparams(("arbitrary",)),
        name="prompt_conv",
    )(u, u, cw, cb, lnw, lnb)


SAMPLE_CONV_ROWS = 40


def _sample_conv_kernel(ucat_ref, wsh_ref, cb_ref, lnw_ref, lnb_ref, o_ref):
    ucat = ucat_ref[...]
    for t in range(DEC_SEQ):
        y = jnp.sum(ucat * wsh_ref[t][None], axis=1)
        o_ref[t] = _ln_swish(y + cb_ref[...], lnw_ref[...], lnb_ref[...])


def _sample_conv(ucat, wsh, cb, lnw, lnb):
    nb = ucat.shape[0]
    return pl.pallas_call(
        _sample_conv_kernel,
        out_shape=jax.ShapeDtypeStruct((DEC_SEQ, nb, CONV_CH), F32),
        compiler_params=pltpu.CompilerParams(vmem_limit_bytes=VMEM_LIMIT),
        name="sample_conv",
    )(ucat, wsh, cb, lnw, lnb)


def _out_proj_kernel(x_ref, o0_ref, o1_ref, o2_ref, l0_ref, l1_ref, l2_ref, cv_ref, w_ref, y_ref, a_scr):
    o_refs = (o0_ref, o1_ref, o2_ref)
    for h in range(HEADS_PER_GROUP):
        l0, l1, l2 = l0_ref[h], l1_ref[h], l2_ref[h]
        lm = jnp.maximum(jnp.maximum(l0, l1), l2)
        es = (jnp.exp(l0 - lm), jnp.exp(l1 - lm), jnp.exp(l2 - lm))
        den = es[0] + es[1] + es[2]
        for g in range(N_GROUPS):
            c0 = g * GROUP_W + h * HEAD_DIM
            a_scr[:, c0:c0 + HEAD_DIM] = ((es[g] / den) * o_refs[g][h]).astype(BF16)
    a_scr[:, ATTN_W:] = cv_ref[...].astype(BF16)
    y_ref[...] = x_ref[...] + _dot(a_scr[...], w_ref[...])


def _out_proj(x, os_, ls_, gidx, conv, w_out_bf, tm):
    m = x.shape[0]
    gspecs = [pl.BlockSpec((HEADS_PER_GROUP, tm, HEAD_DIM), lambda i, gi=gi: (gi, i, 0)) for gi in gidx]
    return pl.pallas_call(
        _out_proj_kernel,
        grid=(m // tm,),
        in_specs=[pl.BlockSpec((tm, D_MODEL), lambda i: (i, 0))] + gspecs + gspecs
        + [pl.BlockSpec((tm, CONV_CH), lambda i: (i, 0)), pl.BlockSpec((D_MODEL, D_MODEL), lambda i: (0, 0))],
        out_specs=pl.BlockSpec((tm, D_MODEL), lambda i: (i, 0)),
        out_shape=jax.ShapeDtypeStruct((m, D_MODEL), F32),
        scratch_shapes=[pltpu.VMEM((tm, D_MODEL), BF16)],
        compiler_params=_cparams(("arbitrary",)),
        name="out_proj",
    )(x, *os_, *ls_, conv, w_out_bf)


def _memory_kv_kernel(mem_ref, nw_ref, wk_ref, wv_ref, kn_ref, k_ref, v_ref):
    h = _rms(mem_ref[...], nw_ref[...]).astype(BF16)
    k = _dot(h, wk_ref[...])
    v_ref[...] = _dot(h, wv_ref[...])
    for hd in range(MEM_W // HEAD_DIM):
        hs = slice(hd * HEAD_DIM, (hd + 1) * HEAD_DIM)
        k_ref[:, hs] = _rms(k[:, hs], kn_ref[...])


def _memory_kv(mem, nw, wk_bf, wv_bf, kn):
    return pl.pallas_call(
        _memory_kv_kernel,
        out_shape=[jax.ShapeDtypeStruct((MEM_TOKENS, MEM_W), F32)] * 2,
        compiler_params=pltpu.CompilerParams(vmem_limit_bytes=VMEM_LIMIT),
        name="memory_kv",
    )(mem, nw, wk_bf, wv_bf, kn)


MEM_HEADS = MEM_W // HEAD_DIM


def _mem_attn_prompt_kernel(x_ref, nw_ref, wq_ref, qn_ref, k_ref, v_ref, wo_ref, y_ref, o_scr):
    x = x_ref[...]
    q = _dot(_rms(x, nw_ref[...]).astype(BF16), wq_ref[...])
    for h in range(MEM_HEADS):
        hs = slice(h * HEAD_DIM, (h + 1) * HEAD_DIM)
        qh = _rms(q[:, hs], qn_ref[...]).astype(BF16)
        s = _dot_nt(qh, k_ref[:, hs].astype(BF16)) * SCALE
        o_scr[:, hs] = _softmax_pv(s, v_ref[:, hs].astype(BF16))[0].astype(BF16)
    y_ref[...] = x + _dot(o_scr[...], wo_ref[...])


def _mem_attn_prompt(x, nw, wq_bf, qn, k, v, wo_bf, tm):
    m = x.shape[0]
    kv_spec = pl.BlockSpec((MEM_TOKENS, MEM_W), lambda i: (0, 0))
    return pl.pallas_call(
        _mem_attn_prompt_kernel,
        grid=(m // tm,),
        in_specs=[
            pl.BlockSpec((tm, D_MODEL), lambda i: (i, 0)),
            pl.BlockSpec((1, D_MODEL), lambda i: (0, 0)),
            pl.BlockSpec((D_MODEL, MEM_W), lambda i: (0, 0)),
            pl.BlockSpec((1, HEAD_DIM), lambda i: (0, 0)),
            kv_spec, kv_spec,
            pl.BlockSpec((MEM_W, D_MODEL), lambda i: (0, 0)),
        ],
        out_specs=pl.BlockSpec((tm, D_MODEL), lambda i: (i, 0)),
        out_shape=jax.ShapeDtypeStruct((m, D_MODEL), F32),
        scratch_shapes=[pltpu.VMEM((tm, MEM_W), BF16)],
        compiler_params=_cparams(("arbitrary",)),
        name="mem_attn_prompt",
    )(x, nw, wq_bf, qn, k, v, wo_bf)


MEM_BB = 8
MEM_ROWS = MEM_BB * DEC_SEQ
MEM_KEYS = MEM_BB * MEM_TOKENS * MEM_HEADS


def _mem_attn_sample_kernel(x_ref, nw_ref, wq_ref, qn_ref, k_ref, v_ref, bias_ref, wo_ref, y_ref, o_scr):
    x = x_ref[...]
    q = _dot(_rms(x, nw_ref[...]).astype(BF16), wq_ref[...])
    qs = jnp.concatenate([_rms(q[:, h * HEAD_DIM:(h + 1) * HEAD_DIM], qn_ref[...]) for h in range(MEM_HEADS)],
                         axis=0).astype(BF16)
    s = _dot_nt(qs, k_ref[...].astype(BF16)) * SCALE + bias_ref[...]
    o = _softmax_pv(s, v_ref[...].astype(BF16))[0]
    for h in range(MEM_HEADS):
        o_scr[:, h * HEAD_DIM:(h + 1) * HEAD_DIM] = o[h * MEM_ROWS:(h + 1) * MEM_ROWS].astype(BF16)
    y_ref[...] = x + _dot(o_scr[...], wo_ref[...])


def _mem_sample_bias():
    r = np.arange(MEM_HEADS * MEM_ROWS)[:, None]
    c = np.arange(MEM_KEYS)[None, :]
    ok = (r // MEM_ROWS == c % MEM_HEADS) & ((r % MEM_ROWS) // DEC_SEQ == c // (MEM_TOKENS * MEM_HEADS))
    return np.where(ok, 0.0, NEG_INF).astype(np.float32)


def _mem_attn_sample(x, nw, wq_bf, qn, k_flat, v_flat, wo_bf):
    m = x.shape[0]
    bias = jnp.asarray(_mem_sample_bias())
    kv_spec = pl.BlockSpec((MEM_KEYS, HEAD_DIM), lambda i: (i, 0))
    return pl.pallas_call(
        _mem_attn_sample_kernel,
        grid=(m // MEM_ROWS,),
        in_specs=[
            pl.BlockSpec((MEM_ROWS, D_MODEL), lambda i: (i, 0)),
            pl.BlockSpec((1, D_MODEL), lambda i: (0, 0)),
            pl.BlockSpec((D_MODEL, MEM_W), lambda i: (0, 0)),
            pl.BlockSpec((1, HEAD_DIM), lambda i: (0, 0)),
            kv_spec, kv_spec,
            pl.BlockSpec(bias.shape, lambda i: (0, 0)),
            pl.BlockSpec((MEM_W, D_MODEL), lambda i: (0, 0)),
        ],
        out_specs=pl.BlockSpec((MEM_ROWS, D_MODEL), lambda i: (i, 0)),
        out_shape=jax.ShapeDtypeStruct((m, D_MODEL), F32),
        scratch_shapes=[pltpu.VMEM((MEM_ROWS, MEM_W), BF16)],
        compiler_params=_cparams(("arbitrary",)),
        name="mem_attn_sample",
    )(x, nw, wq_bf, qn, k_flat, v_flat, bias, wo_bf)


PEER_TQ = 128
NOT_TOP = 99.0
CAND_ROWS = 16 + 7 * 8 + 8
INVALID_FLAT = 999.0


def _cand_flat_index():
    flat = np.full((CAND_ROWS,), INVALID_FLAT, np.float32)
    flat[0:16] = np.arange(16)
    for a in range(1, 8):
        nb = TOPK // (a + 1)
        flat[16 + 8 * (a - 1):16 + 8 * (a - 1) + nb] = a * TOPK + np.arange(nb)
    flat[72:80] = np.arange(8, 16) * TOPK
    return np.broadcast_to(flat[:, None], (CAND_ROWS, PEER_TQ)).copy()


def _topk_rows(s, row, vals_scr):
    rank = jnp.full(s.shape, NOT_TOP, F32)
    for a in range(TOPK):
        m = jnp.max(s, axis=0, keepdims=True)
        idx = jnp.min(jnp.where(s == m, row, float(N_KEYS)), axis=0, keepdims=True)
        hit = row == idx
        rank = jnp.where(hit, float(a), rank)
        s = jnp.where(hit, NEG_INF, s)
        vals_scr[a:a + 1, :] = m
    return rank


def _peer_query_kernel(x_ref, nw_ref, wq_ref, k1_ref, k2_ref, flat_ref,
                       h3t_ref, r2_ref, e2_ref, n1_ref, c1_ref, q_scr, v1_scr, v2_scr):
    h3 = _rms(x_ref[...], nw_ref[...])
    h3t_ref[...] = h3.T.astype(BF16)
    q_scr[...] = _dot(h3.astype(BF16), wq_ref[...]).astype(BF16)
    row = lax.broadcasted_iota(jnp.int32, (N_KEYS, PEER_TQ), 0).astype(F32)
    flat = flat_ref[...]
    cand_ok = flat < INVALID_FLAT

    def head(h, carry):
        off = pl.multiple_of(h * 2 * HEAD_DIM, 2 * HEAD_DIM)
        s1 = _dot_nt(k1_ref[h], q_scr[:, pl.ds(off, HEAD_DIM)])
        s2 = _dot_nt(k2_ref[h], q_scr[:, pl.ds(off + HEAD_DIM, HEAD_DIM)])
        rank1 = _topk_rows(s1, row, v1_scr)
        rank2 = _topk_rows(s2, row, v2_scr)
        pieces = [v1_scr[0:1, :] + v2_scr[...]]
        for a in range(1, 8):
            pieces.append(v1_scr[a:a + 1, :] + v2_scr[0:8, :])
        pieces.append(v1_scr[8:16, :] + v2_scr[0:1, :])
        cand0 = jnp.where(cand_ok, jnp.concatenate(pieces, axis=0), NEG_INF)
        cand = cand0
        for _ in range(TOPK):
            m = jnp.max(cand, axis=0, keepdims=True)
            f = jnp.min(jnp.where(cand == m, flat, 2 * INVALID_FLAT), axis=0, keepdims=True)
            cand = jnp.where(flat == f, NEG_INF, cand)
        sel = cand_ok & (cand == NEG_INF)
        z = jnp.sum(jnp.where(sel, jnp.exp(cand0 - cand0[0:1, :]), 0.0), axis=0, keepdims=True)
        self_ = sel.astype(F32)
        n1 = jnp.zeros((N_KEYS, PEER_TQ), F32)
        for a in range(TOPK):
            if a == 0:
                na = jnp.sum(self_[0:16, :], axis=0, keepdims=True)
            elif a < 8:
                na = jnp.sum(self_[16 + 8 * (a - 1):16 + 8 * a, :], axis=0, keepdims=True)
            else:
                na = self_[72 + a - 8:72 + a - 7, :]
            n1 = jnp.where(rank1 == float(a), na, n1)
        r2_ref[h] = rank2.astype(BF16)
        e2_ref[h] = jnp.exp(s2 - v2_scr[0:1, :]).astype(BF16)
        n1_ref[h] = n1
        c1_ref[h] = jnp.exp(s1 - v1_scr[0:1, :]) / z
        return carry

    lax.fori_loop(0, PEER_HEADS, head, 0)


def _peer_query(x2, nw, wq_bf, k1_bf, k2_bf):
    m = x2.shape[0]
    tq = PEER_TQ
    flat = jnp.asarray(_cand_flat_index())
    key_spec = pl.BlockSpec((PEER_HEADS, N_KEYS, HEAD_DIM), lambda i: (0, 0, 0))
    stat_spec = pl.BlockSpec((PEER_HEADS, N_KEYS, tq), lambda i: (0, 0, i))
    stat_shape = lambda dt: jax.ShapeDtypeStruct((PEER_HEADS, N_KEYS, m), dt)
    return pl.pallas_call(
        _peer_query_kernel,
        grid=(m // tq,),
        in_specs=[
            pl.BlockSpec((tq, D_MODEL), lambda i: (i, 0)),
            pl.BlockSpec((1, D_MODEL), lambda i: (0, 0)),
            pl.BlockSpec((D_MODEL, 2 * HEAD_DIM * PEER_HEADS), lambda i: (0, 0)),
            key_spec, key_spec,
            pl.BlockSpec((CAND_ROWS, tq), lambda i: (0, 0)),
        ],
        out_specs=[pl.BlockSpec((D_MODEL, tq), lambda i: (0, i))] + [stat_spec] * 4,
        out_shape=[jax.ShapeDtypeStruct((D_MODEL, m), BF16)] + [stat_shape(BF16)] * 2 + [stat_shape(F32)] * 2,
        scratch_shapes=[
            pltpu.VMEM((tq, 2 * HEAD_DIM * PEER_HEADS), BF16),
            pltpu.VMEM((TOPK, tq), F32),
            pltpu.VMEM((TOPK, tq), F32),
        ],
        compiler_params=_cparams(("arbitrary",)),
        name="peer_query",
    )(x2, nw, wq_bf, k1_bf, k2_bf, flat)


PEER_TM = 512
PEER_TE = 512
KEYS_PER_TILE = PEER_TE // N_KEYS
N_TILES = N_EXPERTS // PEER_TE
LANE = 128
BF16_ROWS = 16
INV_SQRT2 = 0.7071067811865476


def _gate_lanes(act_ref, wt_ref, r2_ref, e2_ref, n1_ref, c1_ref, key0, lc):
    ls = slice(lc * LANE, (lc + 1) * LANE)
    shape3 = (N_KEYS // BF16_ROWS, BF16_ROWS, LANE)
    gates = [jnp.zeros(shape3, BF16) for _ in range(KEYS_PER_TILE)]
    for h in range(PEER_HEADS):
        r2 = r2_ref[h, :, :, ls]
        e2 = e2_ref[h, :, :, ls]
        for c in range(KEYS_PER_TILE):
            n1 = jnp.broadcast_to(n1_ref[h, key0 + c:key0 + c + 1, ls], shape3[1:]).astype(BF16)
            c1 = jnp.broadcast_to(c1_ref[h, key0 + c:key0 + c + 1, ls], shape3[1:]).astype(BF16)
            gates[c] = gates[c] + (jnp.clip(n1 - r2, 0.0, 1.0) * e2) * c1
    for c in range(KEYS_PER_TILE):
        a = act_ref[c * N_KEYS:(c + 1) * N_KEYS, ls]
        gelu = 0.5 * a * (1.0 + lax.erf(a * INV_SQRT2))
        wt_ref[c * N_KEYS:(c + 1) * N_KEYS, ls] = (gates[c] * gelu.astype(BF16).reshape(shape3)).reshape(N_KEYS, LANE)


ACT_HALF = PEER_TM // 2
OUT_ROWS = D_MODEL // 4


def _pipeline_stages(h3t_ref, acc_scr, stats, *, u_ref, act_out, act_in, key0, wt_out, wt_in, vt_ref):
    def act_piece(n):
        ls = slice(n * ACT_HALF, (n + 1) * ACT_HALF)
        act_out[:, ls] = _dot(u_ref[...], h3t_ref[:, ls])

    def out_piece(r):
        rs = slice(r * OUT_ROWS, (r + 1) * OUT_ROWS)
        acc_scr[rs, :] += _dot(vt_ref[rs, :], wt_in[...])

    def gate_piece(lc):
        _gate_lanes(act_in, wt_out, *stats, key0, lc)

    act_piece(0)
    gate_piece(0)
    out_piece(0)
    gate_piece(1)
    act_piece(1)
    out_piece(1)
    gate_piece(2)
    out_piece(2)
    gate_piece(3)
    out_piece(3)


def _peer_dense_kernel(h3t_ref, u0_ref, u_ref, vt_ref, vtl_ref, r2_ref, e2_ref, n1_ref, c1_ref,
                       x2_ref, y_ref, acc_scr, act0, act1, wt0, wt1):
    j = pl.program_id(1)
    stats = (r2_ref, e2_ref, n1_ref, c1_ref)

    @pl.when(j == 0)
    def _():
        acc_scr[...] = jnp.zeros_like(acc_scr)
        wt1[...] = jnp.zeros_like(wt1)
        act0[...] = _dot(u0_ref[...], h3t_ref[...])

    @pl.when(j % 2 == 0)
    def _():
        _pipeline_stages(h3t_ref, acc_scr, stats, u_ref=u_ref, act_out=act1, act_in=act0, key0=0,
                         wt_out=wt0, wt_in=wt1, vt_ref=vt_ref)

    @pl.when(j % 2 == 1)
    def _():
        _pipeline_stages(h3t_ref, acc_scr, stats, u_ref=u_ref, act_out=act0, act_in=act1, key0=KEYS_PER_TILE,
                         wt_out=wt1, wt_in=wt0, vt_ref=vt_ref)

    @pl.when(j == pl.num_programs(1) - 1)
    def _():
        y_ref[...] = x2_ref[...] + (acc_scr[...] + _dot(vtl_ref[...], wt1[...])).T


def _peer_dense(h3t, u_bf, vt_bf, r2, e2, n1, c1, x2):
    m = x2.shape[0]
    tm, te = PEER_TM, PEER_TE
    last = N_TILES - 1
    once = pl.Buffered(1)
    key2_shape = (PEER_HEADS, N_KEYS // BF16_ROWS, BF16_ROWS, m)
    r2, e2 = r2.reshape(key2_shape), e2.reshape(key2_shape)
    key2_spec = pl.BlockSpec(key2_shape[:3] + (tm,), lambda i, j: (0, 0, 0, i), pipeline_mode=once)
    key1_spec = pl.BlockSpec((PEER_HEADS, 2 * KEYS_PER_TILE, tm), lambda i, j: (0, j // 2, i))
    u_blk, vt_blk = (te, D_MODEL), (D_MODEL, te)
    return pl.pallas_call(
        _peer_dense_kernel,
        grid=(m // tm, N_TILES),
        in_specs=[
            pl.BlockSpec((D_MODEL, tm), lambda i, j: (0, i)),
            pl.BlockSpec(u_blk, lambda i, j: (0, 0), pipeline_mode=once),
            pl.BlockSpec(u_blk, lambda i, j: (jnp.minimum(j + 1, last), 0)),
            pl.BlockSpec(vt_blk, lambda i, j: (0, jnp.maximum(j - 1, 0))),
            pl.BlockSpec(vt_blk, lambda i, j: (0, last), pipeline_mode=once),
            key2_spec, key2_spec, key1_spec, key1_spec,
            pl.BlockSpec((tm, D_MODEL), lambda i, j: (i, 0), pipeline_mode=once),
        ],
        out_specs=pl.BlockSpec((tm, D_MODEL), lambda i, j: (i, 0)),
        out_shape=jax.ShapeDtypeStruct((m, D_MODEL), F32),
        scratch_shapes=[
            pltpu.VMEM((D_MODEL, tm), F32),
            pltpu.VMEM((te, tm), F32),
            pltpu.VMEM((te, tm), F32),
            pltpu.VMEM((te, tm), BF16),
            pltpu.VMEM((te, tm), BF16),
        ],
        compiler_params=_cparams(("arbitrary", "arbitrary")),
        name="peer_dense",
    )(h3t, u_bf, u_bf, vt_bf, vt_bf, r2, e2, n1, c1, x2)


def _peer(x2, nw, wq_bf, k1_bf, k2_bf, u_bf, vt_bf):
    h3t, r2, e2, n1, c1 = _peer_query(x2, nw, wq_bf, k1_bf, k2_bf)
    return _peer_dense(h3t, u_bf, vt_bf, r2, e2, n1, c1, x2)


SAMPLE_BB = 1
SAMPLE_ROWS = SAMPLE_BB * DEC_SEQ
SAMPLE_Q = HEADS_PER_GROUP * SAMPLE_ROWS
NEW_PAD = 128
NEW_ROWS = DEC_SEQ * HEADS_PER_GROUP
SAMPLE_NKC = tuple(SAMPLE_BB * HEADS_PER_GROUP * n for n in (WINDOWS[0], WINDOWS[1], DEC_SEQ * N_BACK))
ROWS_PER_CHUNK = DILATIONS[2] * HEADS_PER_GROUP


def _sample_bias(g):
    nkc = SAMPLE_NKC[g]
    r = np.arange(SAMPLE_Q)[:, None]
    rh, rb, rt = r // SAMPLE_ROWS, (r % SAMPLE_ROWS) // DEC_SEQ, r % DEC_SEQ
    c = np.arange(nkc)[None, :]
    per_batch = nkc // SAMPLE_BB
    cb, ch = c // per_batch, c % HEADS_PER_GROUP
    if g == 0:
        ok_pos = (c % per_batch) // HEADS_PER_GROUP >= rt
    elif g == 1:
        ok_pos = ((c % per_batch) // HEADS_PER_GROUP) % DILATIONS[1] == rt
    else:
        ok_pos = (c % (DEC_SEQ * HEADS_PER_GROUP)) // HEADS_PER_GROUP == rt
    ok_c = (cb == rb) & (ch == rh) & ok_pos
    n = np.arange(NEW_PAD)[None, :]
    nh, nb, nt = n // SAMPLE_ROWS, (n % SAMPLE_ROWS) // DEC_SEQ, n % DEC_SEQ
    ok_n = (n < SAMPLE_Q) & (nh == rh) & (nb == rb) & ((nt <= rt) if g == 0 else (nt == rt))
    return np.where(np.concatenate([ok_c, ok_n], axis=1), 0.0, NEG_INF).astype(np.float32)


def _sample_attn_kernel(qkv_ref, new_ref, k0_ref, v0_ref, k1_ref, v1_ref, k2_ref, v2_ref, b0_ref, b1_ref, b2_ref,
                        o_ref, lse_ref, ok0_ref, ov0_ref, ok1_ref, ov1_ref, ok2_ref, ov2_ref):
    olds = (k0_ref, v0_ref, k1_ref, v1_ref, k2_ref, v2_ref)
    outs = (ok0_ref, ov0_ref, ok1_ref, ov1_ref, ok2_ref, ov2_ref)
    biases = (b0_ref, b1_ref, b2_ref)

    for c in range(2 * N_GROUPS):
        new = new_ref[0, c * NEW_ROWS:(c + 1) * NEW_ROWS, :]
        if c < 4:
            keep = olds[c].shape[1] - NEW_ROWS
            outs[c][0, 0:keep, :] = olds[c][0, NEW_ROWS:, :]
            outs[c][0, keep:, :] = new
        else:
            keep = ROWS_PER_CHUNK - NEW_ROWS
            last = olds[c].shape[0] - 1
            outs[c][:, 0:keep, :] = olds[c][:, NEW_ROWS:, :]
            outs[c][0:last, keep:, :] = olds[c][1:, 0:NEW_ROWS, :]
            outs[c][last, keep:, :] = new

    pad = jnp.zeros((NEW_PAD - SAMPLE_Q, HEAD_DIM), F32)
    for g in range(N_GROUPS):
        rows = slice(g * SAMPLE_Q, (g + 1) * SAMPLE_Q)
        q, kn, vn = (qkv_ref[0, w * N_GROUPS * SAMPLE_Q + g * SAMPLE_Q:w * N_GROUPS * SAMPLE_Q + (g + 1) * SAMPLE_Q, :]
                     for w in range(3))
        if g < 2:
            kc, vc = olds[2 * g][0], olds[2 * g + 1][0]
        else:
            kc = olds[4][:, 0:NEW_ROWS, :].reshape(SAMPLE_NKC[2], HEAD_DIM)
            vc = olds[5][:, 0:NEW_ROWS, :].reshape(SAMPLE_NKC[2], HEAD_DIM)
        k = jnp.concatenate([kc, kn, pad], axis=0).astype(BF16)
        v = jnp.concatenate([vc, vn, pad], axis=0).astype(BF16)
        s = _dot_nt(q.astype(BF16), k) * SCALE + biases[g][...]
        o, lse = _softmax_pv(s, v)
        o_ref[0, rows, :] = o
        lse_ref[0, rows, :] = jnp.broadcast_to(lse, (SAMPLE_Q, HEAD_DIM))


def _sample_attn(qkv_hm, caches):
    nbatch = qkv_hm.shape[1] // DEC_SEQ
    per_batch = qkv_hm.reshape(N_QKV_HEADS, nbatch, DEC_SEQ, HEAD_DIM).transpose(1, 0, 2, 3)
    new = per_batch[:, N_ATTN_HEADS:].reshape(nbatch, 2, N_GROUPS, HEADS_PER_GROUP, DEC_SEQ, HEAD_DIM)
    new = new.transpose(0, 2, 1, 4, 3, 5).reshape(nbatch, 2 * N_GROUPS * NEW_ROWS, HEAD_DIM)
    args = [per_batch.reshape(nbatch, N_QKV_HEADS * DEC_SEQ, HEAD_DIM), new]
    specs = [pl.BlockSpec((1,) + a.shape[1:], lambda i: (i, 0, 0)) for a in args]
    cache_specs, cache_shapes = [], []
    for g in range(N_GROUPS):
        for cache in caches[2 * g:2 * g + 2]:
            rows = cache.shape[1] * HEADS_PER_GROUP
            if g < 2:
                view = cache.reshape(nbatch, rows, HEAD_DIM)
                cache_specs.append(pl.BlockSpec((1, rows, HEAD_DIM), lambda i: (i, 0, 0)))
            else:
                view = cache.reshape(nbatch * rows // ROWS_PER_CHUNK, ROWS_PER_CHUNK, HEAD_DIM)
                cache_specs.append(pl.BlockSpec((rows // ROWS_PER_CHUNK, ROWS_PER_CHUNK, HEAD_DIM), lambda i: (i, 0, 0)))
            args.append(view)
            cache_shapes.append(jax.ShapeDtypeStruct(view.shape, F32))
    specs += cache_specs
    for g in range(N_GROUPS):
        bias = jnp.asarray(_sample_bias(g))
        args.append(bias)
        specs.append(pl.BlockSpec(bias.shape, lambda i: (0, 0)))
    out_spec = pl.BlockSpec((1, N_GROUPS * SAMPLE_Q, HEAD_DIM), lambda i: (i, 0, 0))
    out_shape = jax.ShapeDtypeStruct((nbatch, N_GROUPS * SAMPLE_Q, HEAD_DIM), F32)
    o, lse, *slid = pl.pallas_call(
        _sample_attn_kernel,
        grid=(nbatch,),
        in_specs=specs,
        out_specs=[out_spec, out_spec] + cache_specs,
        out_shape=[out_shape, out_shape] + cache_shapes,
        compiler_params=_cparams(("arbitrary",)),
        name="sample_attn",
    )(*args)
    head_major = lambda a: a.reshape(nbatch, N_ATTN_HEADS, DEC_SEQ, HEAD_DIM).transpose(1, 0, 2, 3).reshape(
        N_ATTN_HEADS, nbatch * DEC_SEQ, HEAD_DIM)
    return head_major(o), head_major(lse), [s_.reshape(c.shape) for s_, c in zip(slid, caches)]


def kernel(x_prompt, x_sample, mem_prompt, cache_k_w128, cache_v_w128, cache_k_w512, cache_v_w512, cache_k_w2048, cache_v_w2048, state_conv, cache_mem_k, cache_mem_v, norm_mix_w, w_in, q_norm_w, k_norm_w, conv_w, conv_b, conv_ln_w, conv_ln_b, w_out, norm_mem_w, mem_norm_w, wq_mem, wk_mem, wv_mem, qn_mem_w, kn_mem_w, wo_mem, norm_ffn_w, w_peer_q, peer_keys1, peer_keys2, peer_u, peer_v):
    s = x_prompt.shape[1]
    nbatch, t_new = x_sample.shape[:2]
    ms = nbatch * t_new

    row = lambda w: w.reshape(1, -1)
    w_in_bf = w_in.astype(BF16)
    w_out_bf = w_out.astype(BF16)
    wq_mem_bf, wk_mem_bf, wv_mem_bf, wo_mem_bf = (w.astype(BF16) for w in (wq_mem, wk_mem, wv_mem, wo_mem))
    wq_peer_bf = w_peer_q.astype(BF16)
    k1_bf = peer_keys1.astype(BF16)
    k2_bf = peer_keys2.astype(BF16)
    u_bf = peer_u.astype(BF16)
    vt_bf = peer_v.astype(BF16).T
    peer_w = (row(norm_ffn_w), wq_peer_bf, k1_bf, k2_bf, u_bf, vt_bf)
    conv_vecs = (row(conv_b), row(conv_ln_w), row(conv_ln_b))

    xp = x_prompt.reshape(s, D_MODEL)
    cos_p, sin_p = _rope_tables(jnp.arange(s, dtype=jnp.int32))
    qkv_p, u_p = _project(xp, row(norm_mix_w), w_in_bf, row(q_norm_w), row(k_norm_w), cos_p, sin_p, tm=512)
    attn = [_prompt_attn(qkv_p, g) for g in range(N_GROUPS)]
    conv_p = _prompt_conv(u_p, conv_w, *conv_vecs)
    x1_p = _out_proj(xp, [a[0] for a in attn], [a[1] for a in attn], (0, 0, 0), conv_p, w_out_bf, tm=512)
    mem_k, mem_v = _memory_kv(mem_prompt.reshape(MEM_TOKENS, D_MODEL), row(mem_norm_w), wk_mem_bf, wv_mem_bf,
                              row(kn_mem_w))
    x2_p = _mem_attn_prompt(x1_p, row(norm_mem_w), wq_mem_bf, row(qn_mem_w), mem_k, mem_v, wo_mem_bf, tm=512)
    y_p = _peer(x2_p, *peer_w)

    def group_heads(qkv_hm, which, g):
        h0 = which * N_ATTN_HEADS + g * HEADS_PER_GROUP
        return qkv_hm[h0:h0 + HEADS_PER_GROUP]

    p_win = []
    for g, win in enumerate(WINDOWS):
        keep = min(win, s)
        for which in (1, 2):
            heads = group_heads(qkv_p, which, g)[:, s - keep:]
            p_win.append(heads.transpose(1, 0, 2)[None])
    p_conv = u_p[s - (CONV_TAPS - 1):].reshape(1, CONV_TAPS - 1, CONV_CH)
    kv4 = lambda a: a.reshape(1, MEM_TOKENS, MEM_HEADS, HEAD_DIM)

    xs = x_sample.reshape(ms, D_MODEL)
    cos_s, sin_s = _rope_tables(PAST_LEN + jnp.arange(ms, dtype=jnp.int32) % t_new)
    qkv_s, u_s = _project(xs, row(norm_mix_w), w_in_bf, row(q_norm_w), row(k_norm_w), cos_s, sin_s, tm=ms)
    caches = (cache_k_w128, cache_v_w128, cache_k_w512, cache_v_w512, cache_k_w2048, cache_v_w2048)
    o_s, lse_s, s_win = _sample_attn(qkv_s, caches)

    n_state = CONV_TAPS - 1
    ucat = jnp.concatenate(
        [state_conv, u_s.reshape(nbatch, t_new, CONV_CH),
         jnp.zeros((nbatch, SAMPLE_CONV_ROWS - n_state - t_new, CONV_CH), F32)], axis=1)
    wsh = jnp.stack([jnp.pad(conv_w, ((t, SAMPLE_CONV_ROWS - CONV_TAPS - t), (0, 0))) for t in range(t_new)])
    conv_s = _sample_conv(ucat, wsh, *conv_vecs)
    conv_s = conv_s.transpose(1, 0, 2).reshape(ms, CONV_CH)
    s_conv = ucat[:, t_new:t_new + n_state]
    x1_s = _out_proj(xs, [o_s] * N_GROUPS, [lse_s] * N_GROUPS, (0, 1, 2), conv_s, w_out_bf, tm=ms)
    x2_s = _mem_attn_sample(x1_s, row(norm_mem_w), wq_mem_bf, row(qn_mem_w),
                            cache_mem_k.reshape(-1, HEAD_DIM), cache_mem_v.reshape(-1, HEAD_DIM), wo_mem_bf)
    y_s = _peer(x2_s, *peer_w)

    return (y_p.reshape(x_prompt.shape), y_s.reshape(x_sample.shape), *p_win, p_conv, kv4(mem_k), kv4(mem_v),
            *s_win, s_conv)
```

```python
import functools

import numpy as np
import jax
import jax.numpy as jnp
from jax import lax
from jax.experimental import pallas as pl
from jax.experimental.pallas import tpu as pltpu

F32 = jnp.float32
BF16 = jnp.bfloat16

D_MODEL = 2048
HEAD_DIM = 128
HEADS_PER_GROUP = 4
GROUP_W = HEADS_PER_GROUP * HEAD_DIM
DILATIONS = (1, 4, 16)
WINDOWS = (128, 512, 2048)
N_BACK = 128
N_GROUPS = 3
ATTN_W = N_GROUPS * GROUP_W
CONV_CH = 512
CONV_TAPS = 31
QKV_W = 3 * ATTN_W
N_ATTN_HEADS = N_GROUPS * HEADS_PER_GROUP
N_QKV_HEADS = 3 * N_ATTN_HEADS
MEM_TOKENS = 256
MEM_W = 512
PEER_HEADS = 8
N_KEYS = 128
N_EXPERTS = N_KEYS * N_KEYS
TOPK = 16
PAST_LEN = 2048
DEC_SEQ = 4
ROPE_THETA = 10000.0
EPS = 1e-6
SCALE = HEAD_DIM ** -0.5
NEG_INF = float("-inf")

VMEM_LIMIT = 56 * 1024 * 1024


def _cparams(sem):
    return pltpu.CompilerParams(dimension_semantics=sem, vmem_limit_bytes=VMEM_LIMIT)


def _rms(x, w):
    return x * lax.rsqrt(jnp.mean(x * x, axis=-1, keepdims=True) + EPS) * w


def _dot(a, b):
    return jnp.dot(a, b, preferred_element_type=F32)


def _dot_nt(a, b):
    return lax.dot_general(a, b, (((1,), (1,)), ((), ())), preferred_element_type=F32)


def _softmax_pv(s, v):
    m = jnp.max(s, axis=-1, keepdims=True)
    p = jnp.exp(s - m)
    den = jnp.sum(p, axis=-1, keepdims=True)
    return _dot(p.astype(BF16), v) / den, m + jnp.log(den)


def _proj_kernel(x_ref, nw_ref, w_ref, wa_ref, wg_ref, qn_ref, kn_ref, cos_ref, sin_ref,
                 qkv_ref, u_ref, h_scr):
    j = pl.program_id(1)

    @pl.when(j == 0)
    def _():
        h_scr[...] = _rms(x_ref[...], nw_ref[...]).astype(BF16)
        a = _dot(h_scr[...], wa_ref[...])
        g = _dot(h_scr[...], wg_ref[...])
        u_ref[...] = a * jax.nn.sigmoid(g)

    z = _dot(h_scr[...], w_ref[...])

    @pl.when(j < 2 * N_GROUPS)
    def _():
        w = jnp.where(j < N_GROUPS, qn_ref[...], kn_ref[...])
        cos = cos_ref[...]
        sin = sin_ref[...]
        for h in range(HEADS_PER_GROUP):
            y = _rms(z[:, h * HEAD_DIM:(h + 1) * HEAD_DIM], w)
            qkv_ref[h] = y * cos + pltpu.roll(y, HEAD_DIM // 2, 1) * sin

    @pl.when(j >= 2 * N_GROUPS)
    def _():
        for h in range(HEADS_PER_GROUP):
            qkv_ref[h] = z[:, h * HEAD_DIM:(h + 1) * HEAD_DIM]


def _project(x, nw, w_in_bf, qn, kn, cos, sin, tm):
    m = x.shape[0]
    n_tiles = QKV_W // GROUP_W
    return pl.pallas_call(
        _proj_kernel,
        grid=(m // tm, n_tiles),
        in_specs=[
            pl.BlockSpec((tm, D_MODEL), lambda i, j: (i, 0)),
            pl.BlockSpec((1, D_MODEL), lambda i, j: (0, 0)),
            pl.BlockSpec((D_MODEL, GROUP_W), lambda i, j: (0, j)),
            pl.BlockSpec((D_MODEL, GROUP_W), lambda i, j: (0, n_tiles)),
            pl.BlockSpec((D_MODEL, GROUP_W), lambda i, j: (0, n_tiles + 1)),
            pl.BlockSpec((1, HEAD_DIM), lambda i, j: (0, 0)),
            pl.BlockSpec((1, HEAD_DIM), lambda i, j: (0, 0)),
            pl.BlockSpec((tm, HEAD_DIM), lambda i, j: (i, 0)),
            pl.BlockSpec((tm, HEAD_DIM), lambda i, j: (i, 0)),
        ],
        out_specs=[
            pl.BlockSpec((HEADS_PER_GROUP, tm, HEAD_DIM), lambda i, j: (j, i, 0)),
            pl.BlockSpec((tm, GROUP_W), lambda i, j: (i, 0)),
        ],
        out_shape=[
            jax.ShapeDtypeStruct((N_QKV_HEADS, m, HEAD_DIM), F32),
            jax.ShapeDtypeStruct((m, CONV_CH), F32),
        ],
        scratch_shapes=[pltpu.VMEM((tm, D_MODEL), BF16)],
        compiler_params=_cparams(("arbitrary", "arbitrary")),
        name="proj",
    )(x, nw, w_in_bf, w_in_bf, w_in_bf, qn, kn, cos, sin)


def _rope_tables(pos):
    half = HEAD_DIM // 2
    inv = ROPE_THETA ** (-jnp.arange(half, dtype=F32) / half)
    ang = pos.astype(F32)[:, None] * inv[None, :]
    cos = jnp.cos(ang)
    sin = jnp.sin(ang)
    return jnp.concatenate([cos, cos], axis=-1), jnp.concatenate([-sin, sin], axis=-1)


QUERY_BLOCK = 128


def _prompt_attn_kernel(q_ref, k_ref, v_ref, o_ref, lse_ref, kprev, vprev, *, dil):
    i = pl.program_id(0)
    qb = QUERY_BLOCK

    @pl.when(i == 0)
    def _():
        kprev[...] = jnp.zeros_like(kprev)
        vprev[...] = jnp.zeros_like(vprev)

    qi = lax.broadcasted_iota(jnp.int32, (qb, 2 * qb), 0)
    kj = lax.broadcasted_iota(jnp.int32, (qb, 2 * qb), 1)
    valid = (kj >= qi + (qb - N_BACK)) & (kj <= qi + qb) & ((kj >= qb) | (i > 0))
    for h in range(HEADS_PER_GROUP):
        for r in range(dil):
            rows = pl.ds(r, qb, stride=dil) if dil > 1 else pl.ds(0, qb)
            q = q_ref[h, rows, :].astype(BF16)
            k = jnp.concatenate([kprev[h, rows, :], k_ref[h, rows, :]], axis=0).astype(BF16)
            v = jnp.concatenate([vprev[h, rows, :], v_ref[h, rows, :]], axis=0).astype(BF16)
            o, lse = _softmax_pv(jnp.where(valid, _dot_nt(q, k) * SCALE, NEG_INF), v)
            o_ref[h, rows, :] = o
            lse_ref[h, rows, :] = jnp.broadcast_to(lse, (qb, HEAD_DIM))
    kprev[...] = k_ref[...]
    vprev[...] = v_ref[...]


def _prompt_attn(qkv_hm, g):
    s = qkv_hm.shape[1]
    dil = DILATIONS[g]
    sb = QUERY_BLOCK * dil
    blk = (HEADS_PER_GROUP, sb, HEAD_DIM)
    out_spec = pl.BlockSpec(blk, lambda i: (0, i, 0))
    out_shape = jax.ShapeDtypeStruct((HEADS_PER_GROUP, s, HEAD_DIM), F32)
    return pl.pallas_call(
        functools.partial(_prompt_attn_kernel, dil=dil),
        grid=(s // sb,),
        in_specs=[
            pl.BlockSpec(blk, lambda i: (g, i, 0)),
            pl.BlockSpec(blk, lambda i: (N_GROUPS + g, i, 0)),
            pl.BlockSpec(blk, lambda i: (2 * N_GROUPS + g, i, 0)),
        ],
        out_specs=[out_spec, out_spec],
        out_shape=[out_shape, out_shape],
        scratch_shapes=[pltpu.VMEM(blk, F32), pltpu.VMEM(blk, F32)],
        compiler_params=_cparams(("arbitrary",)),
        name=f"prompt_attn_g{g}",
    )(qkv_hm, qkv_hm, qkv_hm)


def _ln_swish(y, lnw, lnb):
    mu = jnp.mean(y, axis=-1, keepdims=True)
    yc = y - mu
    yn = yc * lax.rsqrt(jnp.mean(yc * yc, axis=-1, keepdims=True) + EPS) * lnw + lnb
    return yn * jax.nn.sigmoid(yn)


HALO = 32


def _prompt_conv_kernel(u_ref, halo_ref, cw_ref, cb_ref, lnw_ref, lnb_ref, o_ref, scr):
    i = pl.program_id(0)
    tm = u_ref.shape[0]
    scr[0:HALO, :] = jnp.where(i > 0, halo_ref[...], 0.0)
    scr[HALO:, :] = u_ref[...]
    acc = jnp.zeros((tm, CONV_CH), F32)
    for j in range(CONV_TAPS):
        acc = acc + cw_ref[j:j + 1, :] * scr[pl.ds(j + HALO - (CONV_TAPS - 1), tm), :]
    o_ref[...] = _ln_swish(acc + cb_ref[...], lnw_ref[...], lnb_ref[...])


def _prompt_conv(u, cw, cb, lnw, lnb, tm=512):
    m = u.shape[0]
    vec = pl.BlockSpec((1, CONV_CH), lambda i: (0, 0))
    return pl.pallas_call(
        _prompt_conv_kernel,
        grid=(m // tm,),
        in_specs=[
            pl.BlockSpec((tm, CONV_CH), lambda i: (i, 0)),
            pl.BlockSpec((HALO, CONV_CH), lambda i: (jnp.maximum(i * (tm // HALO) - 1, 0), 0)),
            pl.BlockSpec((CONV_TAPS, CONV_CH), lambda i: (0, 0)),
            vec, vec, vec,
        ],
        out_specs=pl.BlockSpec((tm, CONV_CH), lambda i: (i, 0)),
        out_shape=jax.ShapeDtypeStruct((m, CONV_CH), F32),
        scratch_shapes=[pltpu.VMEM((tm + HALO, CONV_CH), F32)],
        compiler_params=_cparams(("arbitrary",)),
        name="prompt_conv",
    )(u, u, cw, cb, lnw, lnb)


SAMPLE_CONV_ROWS = 40


def _sample_conv_kernel(ucat_ref, wsh_ref, cb_ref, lnw_ref, lnb_ref, o_ref):
    ucat = ucat_ref[...]
    for t in range(DEC_SEQ):
        y = jnp.sum(ucat * wsh_ref[t][None], axis=1)
        o_ref[t] = _ln_swish(y + cb_ref[...], lnw_ref[...], lnb_ref[...])


def _sample_conv(ucat, wsh, cb, lnw, lnb):
    nb = ucat.shape[0]
    return pl.pallas_call(
        _sample_conv_kernel,
        out_shape=jax.ShapeDtypeStruct((DEC_SEQ, nb, CONV_CH), F32),
        compiler_params=pltpu.CompilerParams(vmem_limit_bytes=VMEM_LIMIT),
        name="sample_conv",
    )(ucat, wsh, cb, lnw, lnb)


def _out_proj_kernel(x_ref, o0_ref, o1_ref, o2_ref, l0_ref, l1_ref, l2_ref, cv_ref, w_ref, y_ref, a_scr):
    o_refs = (o0_ref, o1_ref, o2_ref)
    for h in range(HEADS_PER_GROUP):
        l0, l1, l2 = l0_ref[h], l1_ref[h], l2_ref[h]
        lm = jnp.maximum(jnp.maximum(l0, l1), l2)
        es = (jnp.exp(l0 - lm), jnp.exp(l1 - lm), jnp.exp(l2 - lm))
        den = es[0] + es[1] + es[2]
        for g in range(N_GROUPS):
            c0 = g * GROUP_W + h * HEAD_DIM
            a_scr[:, c0:c0 + HEAD_DIM] = ((es[g] / den) * o_refs[g][h]).astype(BF16)
    a_scr[:, ATTN_W:] = cv_ref[...].astype(BF16)
    y_ref[...] = x_ref[...] + _dot(a_scr[...], w_ref[...])


def _out_proj(x, os_, ls_, gidx, conv, w_out_bf, tm):
    m = x.shape[0]
    gspecs = [pl.BlockSpec((HEADS_PER_GROUP, tm, HEAD_DIM), lambda i, gi=gi: (gi, i, 0)) for gi in gidx]
    return pl.pallas_call(
        _out_proj_kernel,
        grid=(m // tm,),
        in_specs=[pl.BlockSpec((tm, D_MODEL), lambda i: (i, 0))] + gspecs + gspecs
        + [pl.BlockSpec((tm, CONV_CH), lambda i: (i, 0)), pl.BlockSpec((D_MODEL, D_MODEL), lambda i: (0, 0))],
        out_specs=pl.BlockSpec((tm, D_MODEL), lambda i: (i, 0)),
        out_shape=jax.ShapeDtypeStruct((m, D_MODEL), F32),
        scratch_shapes=[pltpu.VMEM((tm, D_MODEL), BF16)],
        compiler_params=_cparams(("arbitrary",)),
        name="out_proj",
    )(x, *os_, *ls_, conv, w_out_bf)


def _memory_kv_kernel(mem_ref, nw_ref, wk_ref, wv_ref, kn_ref, k_ref, v_ref):
    h = _rms(mem_ref[...], nw_ref[...]).astype(BF16)
    k = _dot(h, wk_ref[...])
    v_ref[...] = _dot(h, wv_ref[...])
    for hd in range(MEM_W // HEAD_DIM):
        hs = slice(hd * HEAD_DIM, (hd + 1) * HEAD_DIM)
        k_ref[:, hs] = _rms(k[:, hs], kn_ref[...])


def _memory_kv(mem, nw, wk_bf, wv_bf, kn):
    return pl.pallas_call(
        _memory_kv_kernel,
        out_shape=[jax.ShapeDtypeStruct((MEM_TOKENS, MEM_W), F32)] * 2,
        compiler_params=pltpu.CompilerParams(vmem_limit_bytes=VMEM_LIMIT),
        name="memory_kv",
    )(mem, nw, wk_bf, wv_bf, kn)


MEM_HEADS = MEM_W // HEAD_DIM


def _mem_attn_prompt_kernel(x_ref, nw_ref, wq_ref, qn_ref, k_ref, v_ref, wo_ref, y_ref, o_scr):
    x = x_ref[...]
    q = _dot(_rms(x, nw_ref[...]).astype(BF16), wq_ref[...])
    for h in range(MEM_HEADS):
        hs = slice(h * HEAD_DIM, (h + 1) * HEAD_DIM)
        qh = _rms(q[:, hs], qn_ref[...]).astype(BF16)
        s = _dot_nt(qh, k_ref[:, hs].astype(BF16)) * SCALE
        o_scr[:, hs] = _softmax_pv(s, v_ref[:, hs].astype(BF16))[0].astype(BF16)
    y_ref[...] = x + _dot(o_scr[...], wo_ref[...])


def _mem_attn_prompt(x, nw, wq_bf, qn, k, v, wo_bf, tm):
    m = x.shape[0]
    kv_spec = pl.BlockSpec((MEM_TOKENS, MEM_W), lambda i: (0, 0))
    return pl.pallas_call(
        _mem_attn_prompt_kernel,
        grid=(m // tm,),
        in_specs=[
            pl.BlockSpec((tm, D_MODEL), lambda i: (i, 0)),
            pl.BlockSpec((1, D_MODEL), lambda i: (0, 0)),
            pl.BlockSpec((D_MODEL, MEM_W), lambda i: (0, 0)),
            pl.BlockSpec((1, HEAD_DIM), lambda i: (0, 0)),
            kv_spec, kv_spec,
            pl.BlockSpec((MEM_W, D_MODEL), lambda i: (0, 0)),
        ],
        out_specs=pl.BlockSpec((tm, D_MODEL), lambda i: (i, 0)),
        out_shape=jax.ShapeDtypeStruct((m, D_MODEL), F32),
        scratch_shapes=[pltpu.VMEM((tm, MEM_W), BF16)],
        compiler_params=_cparams(("arbitrary",)),
        name="mem_attn_prompt",
    )(x, nw, wq_bf, qn, k, v, wo_bf)


MEM_BB = 8
MEM_ROWS = MEM_BB * DEC_SEQ
MEM_KEYS = MEM_BB * MEM_TOKENS * MEM_HEADS


def _mem_attn_sample_kernel(x_ref, nw_ref, wq_ref, qn_ref, k_ref, v_ref, bias_ref, wo_ref, y_ref, o_scr):
    x = x_ref[...]
    q = _dot(_rms(x, nw_ref[...]).astype(BF16), wq_ref[...])
    qs = jnp.concatenate([_rms(q[:, h * HEAD_DIM:(h + 1) * HEAD_DIM], qn_ref[...]) for h in range(MEM_HEADS)],
                         axis=0).astype(BF16)
    s = _dot_nt(qs, k_ref[...].astype(BF16)) * SCALE + bias_ref[...]
    o = _softmax_pv(s, v_ref[...].astype(BF16))[0]
    for h in range(MEM_HEADS):
        o_scr[:, h * HEAD_DIM:(h + 1) * HEAD_DIM] = o[h * MEM_ROWS:(h + 1) * MEM_ROWS].astype(BF16)
    y_ref[...] = x + _dot(o_scr[...], wo_ref[...])


def _mem_sample_bias():
    r = np.arange(MEM_HEADS * MEM_ROWS)[:, None]
    c = np.arange(MEM_KEYS)[None, :]
    ok = (r // MEM_ROWS == c % MEM_HEADS) & ((r % MEM_ROWS) // DEC_SEQ == c // (MEM_TOKENS * MEM_HEADS))
    return np.where(ok, 0.0, NEG_INF).astype(np.float32)


def _mem_attn_sample(x, nw, wq_bf, qn, k_flat, v_flat, wo_bf):
    m = x.shape[0]
    bias = jnp.asarray(_mem_sample_bias())
    kv_spec = pl.BlockSpec((MEM_KEYS, HEAD_DIM), lambda i: (i, 0))
    return pl.pallas_call(
        _mem_attn_sample_kernel,
        grid=(m // MEM_ROWS,),
        in_specs=[
            pl.BlockSpec((MEM_ROWS, D_MODEL), lambda i: (i, 0)),
            pl.BlockSpec((1, D_MODEL), lambda i: (0, 0)),
            pl.BlockSpec((D_MODEL, MEM_W), lambda i: (0, 0)),
            pl.BlockSpec((1, HEAD_DIM), lambda i: (0, 0)),
            kv_spec, kv_spec,
            pl.BlockSpec(bias.shape, lambda i: (0, 0)),
            pl.BlockSpec((MEM_W, D_MODEL), lambda i: (0, 0)),
        ],
        out_specs=pl.BlockSpec((MEM_ROWS, D_MODEL), lambda i: (i, 0)),
        out_shape=jax.ShapeDtypeStruct((m, D_MODEL), F32),
        scratch_shapes=[pltpu.VMEM((MEM_ROWS, MEM_W), BF16)],
        compiler_params=_cparams(("arbitrary",)),
        name="mem_attn_sample",
    )(x, nw, wq_bf, qn, k_flat, v_flat, bias, wo_bf)


PEER_TQ = 128
NOT_TOP = 99.0
CAND_ROWS = 16 + 7 * 8 + 8
INVALID_FLAT = 999.0


def _cand_flat_index():
    flat = np.full((CAND_ROWS,), INVALID_FLAT, np.float32)
    flat[0:16] = np.arange(16)
    for a in range(1, 8):
        nb = TOPK // (a + 1)
        flat[16 + 8 * (a - 1):16 + 8 * (a - 1) + nb] = a * TOPK + np.arange(nb)
    flat[72:80] = np.arange(8, 16) * TOPK
    return np.broadcast_to(flat[:, None], (CAND_ROWS, PEER_TQ)).copy()


def _topk_rows(sa, sb, row, va_scr, vb_scr, ia_scr, ib_scr):
    for a in range(TOPK):
        ma = jnp.max(sa, axis=0, keepdims=True)
        mb = jnp.max(sb, axis=0, keepdims=True)
        ia = jnp.min(jnp.where(sa == ma, row, float(N_KEYS)), axis=0, keepdims=True)
        ib = jnp.min(jnp.where(sb == mb, row, float(N_KEYS)), axis=0, keepdims=True)
        sa = jnp.where(row == ia, NEG_INF, sa)
        sb = jnp.where(row == ib, NEG_INF, sb)
        va_scr[a:a + 1, :] = ma
        vb_scr[a:a + 1, :] = mb
        ia_scr[a:a + 1, :] = ia
        ib_scr[a:a + 1, :] = ib


def _peer_query_kernel(x_ref, nw_ref, wq_ref, k1_ref, k2_ref, flat_ref,
                       h3t_ref, r2_ref, e2_ref, n1_ref, c1_ref, q_scr, v1_scr, v2_scr, i1_scr, i2_scr):
    h3 = _rms(x_ref[...], nw_ref[...])
    h3t_ref[...] = h3.T.astype(BF16)
    q_scr[...] = _dot(h3.astype(BF16), wq_ref[...]).astype(BF16)
    row = lax.broadcasted_iota(jnp.int32, (N_KEYS, PEER_TQ), 0).astype(F32)
    flat = flat_ref[...]
    cand_ok = flat < INVALID_FLAT

    def head(h, carry):
        off = pl.multiple_of(h * 2 * HEAD_DIM, 2 * HEAD_DIM)
        s1 = _dot_nt(k1_ref[h], q_scr[:, pl.ds(off, HEAD_DIM)])
        s2 = _dot_nt(k2_ref[h], q_scr[:, pl.ds(off + HEAD_DIM, HEAD_DIM)])
        _topk_rows(s1, s2, row, v1_scr, v2_scr, i1_scr, i2_scr)
        pieces = [v1_scr[0:1, :] + v2_scr[...]]
        for a in range(1, 8):
            pieces.append(v1_scr[a:a + 1, :] + v2_scr[0:8, :])
        pieces.append(v1_scr[8:16, :] + v2_scr[0:1, :])
        cand0 = jnp.where(cand_ok, jnp.concatenate(pieces, axis=0), NEG_INF)
        cand = cand0
        for _ in range(TOPK):
            m = jnp.max(cand, axis=0, keepdims=True)
            f = jnp.min(jnp.where(cand == m, flat, 2 * INVALID_FLAT), axis=0, keepdims=True)
            cand = jnp.where(flat == f, NEG_INF, cand)
        sel = cand_ok & (cand == NEG_INF)
        z = jnp.sum(jnp.where(sel, jnp.exp(cand0 - cand0[0:1, :]), 0.0), axis=0, keepdims=True)
        self_ = sel.astype(F32)
        n1 = jnp.zeros((N_KEYS, PEER_TQ), F32)
        for a in range(TOPK):
            if a == 0:
                na = jnp.sum(self_[0:16, :], axis=0, keepdims=True)
            elif a < 8:
                na = jnp.sum(self_[16 + 8 * (a - 1):16 + 8 * a, :], axis=0, keepdims=True)
            else:
                na = self_[72 + a - 8:72 + a - 7, :]
            n1 = jnp.where(row == i1_scr[a:a + 1, :], na, n1)
        rank2 = jnp.full((N_KEYS, PEER_TQ), NOT_TOP, F32)
        for a in range(TOPK):
            rank2 = jnp.where(row == i2_scr[a:a + 1, :], float(a), rank2)
        r2_ref[h] = rank2.astype(BF16)
        e2_ref[h] = jnp.exp(s2 - v2_scr[0:1, :]).astype(BF16)
        n1_ref[h] = n1
        c1_ref[h] = jnp.exp(s1 - v1_scr[0:1, :]) / z
        return carry

    lax.fori_loop(0, PEER_HEADS, head, 0)


def _peer_query(x2, nw, wq_bf, k1_bf, k2_bf):
    m = x2.shape[0]
    tq = PEER_TQ
    flat = jnp.asarray(_cand_flat_index())
    key_spec = pl.BlockSpec((PEER_HEADS, N_KEYS, HEAD_DIM), lambda i: (0, 0, 0))
    stat_spec = pl.BlockSpec((PEER_HEADS, N_KEYS, tq), lambda i: (0, 0, i))
    stat_shape = lambda dt: jax.ShapeDtypeStruct((PEER_HEADS, N_KEYS, m), dt)
    return pl.pallas_call(
        _peer_query_kernel,
        grid=(m // tq,),
        in_specs=[
            pl.BlockSpec((tq, D_MODEL), lambda i: (i, 0)),
            pl.BlockSpec((1, D_MODEL), lambda i: (0, 0)),
            pl.BlockSpec((D_MODEL, 2 * HEAD_DIM * PEER_HEADS), lambda i: (0, 0)),
            key_spec, key_spec,
            pl.BlockSpec((CAND_ROWS, tq), lambda i: (0, 0)),
        ],
        out_specs=[pl.BlockSpec((D_MODEL, tq), lambda i: (0, i))] + [stat_spec] * 4,
        out_shape=[jax.ShapeDtypeStruct((D_MODEL, m), BF16)] + [stat_shape(BF16)] * 2 + [stat_shape(F32)] * 2,
        scratch_shapes=[
            pltpu.VMEM((tq, 2 * HEAD_DIM * PEER_HEADS), BF16),
            pltpu.VMEM((TOPK, tq), F32),
            pltpu.VMEM((TOPK, tq), F32),
            pltpu.VMEM((TOPK, tq), F32),
            pltpu.VMEM((TOPK, tq), F32),
        ],
        compiler_params=_cparams(("arbitrary",)),
        name="peer_query",
    )(x2, nw, wq_bf, k1_bf, k2_bf, flat)


PEER_TM = 512
PEER_TE = 512
KEYS_PER_TILE = PEER_TE // N_KEYS
N_TILES = N_EXPERTS // PEER_TE
LANE = 128
BF16_ROWS = 16
INV_SQRT2 = 0.7071067811865476


def _gate_lanes(act_ref, wt_ref, r2_ref, e2_ref, n1_ref, c1_ref, key0, lc):
    ls = slice(lc * LANE, (lc + 1) * LANE)
    shape3 = (N_KEYS // BF16_ROWS, BF16_ROWS, LANE)
    gates = [jnp.zeros(shape3, BF16) for _ in range(KEYS_PER_TILE)]
    for h in range(PEER_HEADS):
        r2 = r2_ref[h, :, :, ls]
        e2 = e2_ref[h, :, :, ls]
        for c in range(KEYS_PER_TILE):
            n1 = jnp.broadcast_to(n1_ref[h, key0 + c:key0 + c + 1, ls], shape3[1:]).astype(BF16)
            c1 = jnp.broadcast_to(c1_ref[h, key0 + c:key0 + c + 1, ls], shape3[1:]).astype(BF16)
            gates[c] = gates[c] + (jnp.clip(n1 - r2, 0.0, 1.0) * e2) * c1
    for c in range(KEYS_PER_TILE):
        a = act_ref[c * N_KEYS:(c + 1) * N_KEYS, ls]
        gelu = 0.5 * a * (1.0 + lax.erf(a * INV_SQRT2))
        wt_ref[c * N_KEYS:(c + 1) * N_KEYS, ls] = (gates[c] * gelu.astype(BF16).reshape(shape3)).reshape(N_KEYS, LANE)


ACT_HALF = PEER_TM // 2
OUT_ROWS = D_MODEL // 4
assert D_MODEL // OUT_ROWS == PEER_TM // LANE


def _pipeline_stages(h3t_ref, acc_scr, stats, *, u_ref, act_out, act_in, key0, wt_out, wt_in, vt_ref):
    def act_piece(n):
        ls = slice(n * ACT_HALF, (n + 1) * ACT_HALF)
        act_out[:, ls] = _dot(u_ref[...], h3t_ref[:, ls])

    def out_piece(r):
        rs = slice(r * OUT_ROWS, (r + 1) * OUT_ROWS)
        acc_scr[rs, :] += _dot(vt_ref[rs, :], wt_in[...])

    def gate_piece(lc):
        _gate_lanes(act_in, wt_out, *stats, key0, lc)

    for r in range(D_MODEL // OUT_ROWS):
        out_piece(r)
        gate_piece(r)
    act_piece(0)
    act_piece(1)


def _peer_dense_kernel(h3t_ref, u0_ref, u_ref, vt_ref, vtl_ref, r2_ref, e2_ref, n1_ref, c1_ref,
                       x2_ref, y_ref, acc_scr, act0, act1, wt0, wt1):
    j = pl.program_id(1)
    stats = (r2_ref, e2_ref, n1_ref, c1_ref)

    @pl.when(j == 0)
    def _():
        acc_scr[...] = jnp.zeros_like(acc_scr)
        wt1[...] = jnp.zeros_like(wt1)
        act0[...] = _dot(u0_ref[...], h3t_ref[...])

    @pl.when(j % 2 == 0)
    def _():
        _pipeline_stages(h3t_ref, acc_scr, stats, u_ref=u_ref, act_out=act1, act_in=act0, key0=0,
                         wt_out=wt0, wt_in=wt1, vt_ref=vt_ref)

    @pl.when(j % 2 == 1)
    def _():
        _pipeline_stages(h3t_ref, acc_scr, stats, u_ref=u_ref, act_out=act0, act_in=act1, key0=KEYS_PER_TILE,
                         wt_out=wt1, wt_in=wt0, vt_ref=vt_ref)

    @pl.when(j == pl.num_programs(1) - 1)
    def _():
        y_ref[...] = x2_ref[...] + (acc_scr[...] + _dot(vtl_ref[...], wt1[...])).T


def _peer_dense(h3t, u_bf, vt_bf, r2, e2, n1, c1, x2):
    m = x2.shape[0]
    tm, te = PEER_TM, PEER_TE
    last = N_TILES - 1
    once = pl.Buffered(1)
    key2_shape = (PEER_HEADS, N_KEYS // BF16_ROWS, BF16_ROWS, m)
    r2, e2 = r2.reshape(key2_shape), e2.reshape(key2_shape)
    key2_spec = pl.BlockSpec(key2_shape[:3] + (tm,), lambda i, j: (0, 0, 0, i), pipeline_mode=once)
    key1_spec = pl.BlockSpec((PEER_HEADS, 2 * KEYS_PER_TILE, tm), lambda i, j: (0, j // 2, i))
    u_blk, vt_blk = (te, D_MODEL), (D_MODEL, te)
    return pl.pallas_call(
        _peer_dense_kernel,
        grid=(m // tm, N_TILES),
        in_specs=[
            pl.BlockSpec((D_MODEL, tm), lambda i, j: (0, i)),
            pl.BlockSpec(u_blk, lambda i, j: (0, 0), pipeline_mode=once),
            pl.BlockSpec(u_blk, lambda i, j: (jnp.minimum(j + 1, last), 0)),
            pl.BlockSpec(vt_blk, lambda i, j: (0, jnp.maximum(j - 1, 0))),
            pl.BlockSpec(vt_blk, lambda i, j: (0, last), pipeline_mode=once),
            key2_spec, key2_spec, key1_spec, key1_spec,
            pl.BlockSpec((tm, D_MODEL), lambda i, j: (i, 0), pipeline_mode=once),
        ],
        out_specs=pl.BlockSpec((tm, D_MODEL), lambda i, j: (i, 0)),
        out_shape=jax.ShapeDtypeStruct((m, D_MODEL), F32),
        scratch_shapes=[
            pltpu.VMEM((D_MODEL, tm), F32),
            pltpu.VMEM((te, tm), F32),
            pltpu.VMEM((te, tm), F32),
            pltpu.VMEM((te, tm), BF16),
            pltpu.VMEM((te, tm), BF16),
        ],
        compiler_params=_cparams(("arbitrary", "arbitrary")),
        name="peer_dense",
    )(h3t, u_bf, u_bf, vt_bf, vt_bf, r2, e2, n1, c1, x2)


def _peer(x2, nw, wq_bf, k1_bf, k2_bf, u_bf, vt_bf):
    h3t, r2, e2, n1, c1 = _peer_query(x2, nw, wq_bf, k1_bf, k2_bf)
    return _peer_dense(h3t, u_bf, vt_bf, r2, e2, n1, c1, x2)


SAMPLE_BB = 1
SAMPLE_ROWS = SAMPLE_BB * DEC_SEQ
SAMPLE_Q = HEADS_PER_GROUP * SAMPLE_ROWS
NEW_PAD = 128
NEW_ROWS = DEC_SEQ * HEADS_PER_GROUP
SAMPLE_NKC = tuple(SAMPLE_BB * HEADS_PER_GROUP * n for n in (WINDOWS[0], WINDOWS[1], DEC_SEQ * N_BACK))
ROWS_PER_CHUNK = DILATIONS[2] * HEADS_PER_GROUP


def _sample_bias(g):
    nkc = SAMPLE_NKC[g]
    r = np.arange(SAMPLE_Q)[:, None]
    rh, rb, rt = r // SAMPLE_ROWS, (r % SAMPLE_ROWS) // DEC_SEQ, r % DEC_SEQ
    c = np.arange(nkc)[None, :]
    per_batch = nkc // SAMPLE_BB
    cb, ch = c // per_batch, c % HEADS_PER_GROUP
    if g == 0:
        ok_pos = (c % per_batch) // HEADS_PER_GROUP >= rt
    elif g == 1:
        ok_pos = ((c % per_batch) // HEADS_PER_GROUP) % DILATIONS[1] == rt
    else:
        ok_pos = (c % (DEC_SEQ * HEADS_PER_GROUP)) // HEADS_PER_GROUP == rt
    ok_c = (cb == rb) & (ch == rh) & ok_pos
    n = np.arange(NEW_PAD)[None, :]
    nh, nb, nt = n // SAMPLE_ROWS, (n % SAMPLE_ROWS) // DEC_SEQ, n % DEC_SEQ
    ok_n = (n < SAMPLE_Q) & (nh == rh) & (nb == rb) & ((nt <= rt) if g == 0 else (nt == rt))
    return np.where(np.concatenate([ok_c, ok_n], axis=1), 0.0, NEG_INF).astype(np.float32)


def _sample_attn_kernel(qkv_ref, new_ref, k0_ref, v0_ref, k1_ref, v1_ref, k2_ref, v2_ref, b0_ref, b1_ref, b2_ref,
                        o_ref, lse_ref, ok0_ref, ov0_ref, ok1_ref, ov1_ref, ok2_ref, ov2_ref):
    olds = (k0_ref, v0_ref, k1_ref, v1_ref, k2_ref, v2_ref)
    outs = (ok0_ref, ov0_ref, ok1_ref, ov1_ref, ok2_ref, ov2_ref)
    biases = (b0_ref, b1_ref, b2_ref)

    for c in range(2 * N_GROUPS):
        new = new_ref[0, c * NEW_ROWS:(c + 1) * NEW_ROWS, :]
        if c < 4:
            keep = olds[c].shape[1] - NEW_ROWS
            outs[c][0, 0:keep, :] = olds[c][0, NEW_ROWS:, :]
            outs[c][0, keep:, :] = new
        else:
            keep = ROWS_PER_CHUNK - NEW_ROWS
            last = olds[c].shape[0] - 1
            outs[c][:, 0:keep, :] = olds[c][:, NEW_ROWS:, :]
            outs[c][0:last, keep:, :] = olds[c][1:, 0:NEW_ROWS, :]
            outs[c][last, keep:, :] = new

    pad = jnp.zeros((NEW_PAD - SAMPLE_Q, HEAD_DIM), F32)
    for g in range(N_GROUPS):
        rows = slice(g * SAMPLE_Q, (g + 1) * SAMPLE_Q)
        q, kn, vn = (qkv_ref[0, w * N_GROUPS * SAMPLE_Q + g * SAMPLE_Q:w * N_GROUPS * SAMPLE_Q + (g + 1) * SAMPLE_Q, :]
                     for w in range(3))
        if g < 2:
            kc, vc = olds[2 * g][0], olds[2 * g + 1][0]
        else:
            kc = olds[4][:, 0:NEW_ROWS, :].reshape(SAMPLE_NKC[2], HEAD_DIM)
            vc = olds[5][:, 0:NEW_ROWS, :].reshape(SAMPLE_NKC[2], HEAD_DIM)
        k = jnp.concatenate([kc, kn, pad], axis=0).astype(BF16)
        v = jnp.concatenate([vc, vn, pad], axis=0).astype(BF16)
        s = _dot_nt(q.astype(BF16), k) * SCALE + biases[g][...]
        o, lse = _softmax_pv(s, v)
        o_ref[0, rows, :] = o
        lse_ref[0, rows, :] = jnp.broadcast_to(lse, (SAMPLE_Q, HEAD_DIM))


def _sample_attn(qkv_hm, caches):
    nbatch = qkv_hm.shape[1] // DEC_SEQ
    per_batch = qkv_hm.reshape(N_QKV_HEADS, nbatch, DEC_SEQ, HEAD_DIM).transpose(1, 0, 2, 3)
    new = per_batch[:, N_ATTN_HEADS:].reshape(nbatch, 2, N_GROUPS, HEADS_PER_GROUP, DEC_SEQ, HEAD_DIM)
    new = new.transpose(0, 2, 1, 4, 3, 5).reshape(nbatch, 2 * N_GROUPS * NEW_ROWS, HEAD_DIM)
    args = [per_batch.reshape(nbatch, N_QKV_HEADS * DEC_SEQ, HEAD_DIM), new]
    specs = [pl.BlockSpec((1,) + a.shape[1:], lambda i: (i, 0, 0)) for a in args]
    cache_specs, cache_shapes = [], []
    for g in range(N_GROUPS):
        for cache in caches[2 * g:2 * g + 2]:
            rows = cache.shape[1] * HEADS_PER_GROUP
            if g < 2:
                view = cache.reshape(nbatch, rows, HEAD_DIM)
                cache_specs.append(pl.BlockSpec((1, rows, HEAD_DIM), lambda i: (i, 0, 0)))
            else:
                view = cache.reshape(nbatch * rows // ROWS_PER_CHUNK, ROWS_PER_CHUNK, HEAD_DIM)
                cache_specs.append(pl.BlockSpec((rows // ROWS_PER_CHUNK, ROWS_PER_CHUNK, HEAD_DIM), lambda i: (i, 0, 0)))
            args.append(view)
            cache_shapes.append(jax.ShapeDtypeStruct(view.shape, F32))
    specs += cache_specs
    for g in range(N_GROUPS):
        bias = jnp.asarray(_sample_bias(g))
        args.append(bias)
        specs.append(pl.BlockSpec(bias.shape, lambda i: (0, 0)))
    out_spec = pl.BlockSpec((1, N_GROUPS * SAMPLE_Q, HEAD_DIM), lambda i: (i, 0, 0))
    out_shape = jax.ShapeDtypeStruct((nbatch, N_GROUPS * SAMPLE_Q, HEAD_DIM), F32)
    o, lse, *slid = pl.pallas_call(
        _sample_attn_kernel,
        grid=(nbatch,),
        in_specs=specs,
        out_specs=[out_spec, out_spec] + cache_specs,
        out_shape=[out_shape, out_shape] + cache_shapes,
        compiler_params=_cparams(("arbitrary",)),
        name="sample_attn",
    )(*args)
    head_major = lambda a: a.reshape(nbatch, N_ATTN_HEADS, DEC_SEQ, HEAD_DIM).transpose(1, 0, 2, 3).reshape(
        N_ATTN_HEADS, nbatch * DEC_SEQ, HEAD_DIM)
    return head_major(o), head_major(lse), [s_.reshape(c.shape) for s_, c in zip(slid, caches)]


def kernel(x_prompt, x_sample, mem_prompt, cache_k_w128, cache_v_w128, cache_k_w512, cache_v_w512, cache_k_w2048, cache_v_w2048, state_conv, cache_mem_k, cache_mem_v, norm_mix_w, w_in, q_norm_w, k_norm_w, conv_w, conv_b, conv_ln_w, conv_ln_b, w_out, norm_mem_w, mem_norm_w, wq_mem, wk_mem, wv_mem, qn_mem_w, kn_mem_w, wo_mem, norm_ffn_w, w_peer_q, peer_keys1, peer_keys2, peer_u, peer_v):
    s = x_prompt.shape[1]
    nbatch, t_new = x_sample.shape[:2]
    ms = nbatch * t_new

    row = lambda w: w.reshape(1, -1)
    w_in_bf = w_in.astype(BF16)
    w_out_bf = w_out.astype(BF16)
    wq_mem_bf, wk_mem_bf, wv_mem_bf, wo_mem_bf = (w.astype(BF16) for w in (wq_mem, wk_mem, wv_mem, wo_mem))
    wq_peer_bf = w_peer_q.astype(BF16)
    k1_bf = peer_keys1.astype(BF16)
    k2_bf = peer_keys2.astype(BF16)
    u_bf = peer_u.astype(BF16)
    vt_bf = peer_v.astype(BF16).T
    peer_w = (row(norm_ffn_w), wq_peer_bf, k1_bf, k2_bf, u_bf, vt_bf)
    conv_vecs = (row(conv_b), row(conv_ln_w), row(conv_ln_b))

    xp = x_prompt.reshape(s, D_MODEL)
    cos_p, sin_p = _rope_tables(jnp.arange(s, dtype=jnp.int32))
    qkv_p, u_p = _project(xp, row(norm_mix_w), w_in_bf, row(q_norm_w), row(k_norm_w), cos_p, sin_p, tm=1024)
    attn = [_prompt_attn(qkv_p, g) for g in range(N_GROUPS)]
    conv_p = _prompt_conv(u_p, conv_w, *conv_vecs)
    x1_p = _out_proj(xp, [a[0] for a in attn], [a[1] for a in attn], (0, 0, 0), conv_p, w_out_bf, tm=512)
    mem_k, mem_v = _memory_kv(mem_prompt.reshape(MEM_TOKENS, D_MODEL), row(mem_norm_w), wk_mem_bf, wv_mem_bf,
                              row(kn_mem_w))
    x2_p = _mem_attn_prompt(x1_p, row(norm_mem_w), wq_mem_bf, row(qn_mem_w), mem_k, mem_v, wo_mem_bf, tm=512)
    y_p = _peer(x2_p, *peer_w)

    def group_heads(qkv_hm, which, g):
        h0 = which * N_ATTN_HEADS + g * HEADS_PER_GROUP
        return qkv_hm[h0:h0 + HEADS_PER_GROUP]

    p_win = []
    for g, win in enumerate(WINDOWS):
        keep = min(win, s)
        for which in (1, 2):
            heads = group_heads(qkv_p, which, g)[:, s - keep:]
            p_win.append(heads.transpose(1, 0, 2)[None])
    p_conv = u_p[s - (CONV_TAPS - 1):].reshape(1, CONV_TAPS - 1, CONV_CH)
    kv4 = lambda a: a.reshape(1, MEM_TOKENS, MEM_HEADS, HEAD_DIM)

    xs = x_sample.reshape(ms, D_MODEL)
    cos_s, sin_s = _rope_tables(PAST_LEN + jnp.arange(ms, dtype=jnp.int32) % t_new)
    qkv_s, u_s = _project(xs, row(norm_mix_w), w_in_bf, row(q_norm_w), row(k_norm_w), cos_s, sin_s, tm=ms)
    caches = (cache_k_w128, cache_v_w128, cache_k_w512, cache_v_w512, cache_k_w2048, cache_v_w2048)
    o_s, lse_s, s_win = _sample_attn(qkv_s, caches)

    n_state = CONV_TAPS - 1
    ucat = jnp.concatenate(
        [state_conv, u_s.reshape(nbatch, t_new, CONV_CH),
         jnp.zeros((nbatch, SAMPLE_CONV_ROWS - n_state - t_new, CONV_CH), F32)], axis=1)
    wsh = jnp.stack([jnp.pad(conv_w, ((t, SAMPLE_CONV_ROWS - CONV_TAPS - t), (0, 0))) for t in range(t_new)])
    conv_s = _sample_conv(ucat, wsh, *conv_vecs)
    conv_s = conv_s.transpose(1, 0, 2).reshape(ms, CONV_CH)
    s_conv = ucat[:, t_new:t_new + n_state]
    x1_s = _out_proj(xs, [o_s] * N_GROUPS, [lse_s] * N_GROUPS, (0, 1, 2), conv_s, w_out_bf, tm=ms)
    x2_s = _mem_attn_sample(x1_s, row(norm_mem_w), wq_mem_bf, row(qn_mem_w),
                            cache_mem_k.reshape(-1, HEAD_DIM), cache_mem_v.reshape(-1, HEAD_DIM), wo_mem_bf)
    y_s = _peer(x2_s, *peer_w)

    return (y_p.reshape(x_prompt.shape), y_s.reshape(x_sample.shape), *p_win, p_conv, kv4(mem_k), kv4(mem_v),
            *s_win, s_conv)
```

```python
import functools

import numpy as np
import jax
import jax.numpy as jnp
from jax import lax
from jax.experimental import pallas as pl
from jax.experimental.pallas import tpu as pltpu

F32 = jnp.float32
BF16 = jnp.bfloat16

D_MODEL = 2048
HEAD_DIM = 128
HEADS_PER_GROUP = 4
GROUP_W = HEADS_PER_GROUP * HEAD_DIM
DILATIONS = (1, 4, 16)
WINDOWS = (128, 512, 2048)
N_BACK = 128
N_GROUPS = 3
ATTN_W = N_GROUPS * GROUP_W
CONV_CH = 512
CONV_TAPS = 31
QKV_W = 3 * ATTN_W
N_ATTN_HEADS = N_GROUPS * HEADS_PER_GROUP
N_QKV_HEADS = 3 * N_ATTN_HEADS
MEM_TOKENS = 256
MEM_W = 512
PEER_HEADS = 8
N_KEYS = 128
N_EXPERTS = N_KEYS * N_KEYS
TOPK = 16
PAST_LEN = 2048
DEC_SEQ = 4
ROPE_THETA = 10000.0
EPS = 1e-6
SCALE = HEAD_DIM ** -0.5
NEG_INF = float("-inf")

VMEM_LIMIT = 56 * 1024 * 1024


def _cparams(sem):
    return pltpu.CompilerParams(dimension_semantics=sem, vmem_limit_bytes=VMEM_LIMIT)


def _rms(x, w):
    return x * lax.rsqrt(jnp.mean(x * x, axis=-1, keepdims=True) + EPS) * w


def _dot(a, b):
    return jnp.dot(a, b, preferred_element_type=F32)


def _dot_nt(a, b):
    return lax.dot_general(a, b, (((1,), (1,)), ((), ())), preferred_element_type=F32)


def _softmax_pv(s, v):
    m = jnp.max(s, axis=-1, keepdims=True)
    p = jnp.exp(s - m)
    den = jnp.sum(p, axis=-1, keepdims=True)
    return _dot(p.astype(BF16), v) / den, m + jnp.log(den)


def _proj_kernel(x_ref, nw_ref, w_ref, wa_ref, wg_ref, qn_ref, kn_ref, cos_ref, sin_ref,
                 qkv_ref, u_ref, h_scr):
    j = pl.program_id(1)

    @pl.when(j == 0)
    def _():
        h_scr[...] = _rms(x_ref[...], nw_ref[...]).astype(BF16)
        a = _dot(h_scr[...], wa_ref[...])
        g = _dot(h_scr[...], wg_ref[...])
        u_ref[...] = a * jax.nn.sigmoid(g)

    z = _dot(h_scr[...], w_ref[...])

    @pl.when(j < 2 * N_GROUPS)
    def _():
        w = jnp.where(j < N_GROUPS, qn_ref[...], kn_ref[...])
        cos = cos_ref[...]
        sin = sin_ref[...]
        for h in range(HEADS_PER_GROUP):
            y = _rms(z[:, h * HEAD_DIM:(h + 1) * HEAD_DIM], w)
            qkv_ref[h] = y * cos + pltpu.roll(y, HEAD_DIM // 2, 1) * sin

    @pl.when(j >= 2 * N_GROUPS)
    def _():
        for h in range(HEADS_PER_GROUP):
            qkv_ref[h] = z[:, h * HEAD_DIM:(h + 1) * HEAD_DIM]


def _project(x, nw, w_in_bf, qn, kn, cos, sin, tm):
    m = x.shape[0]
    n_tiles = QKV_W // GROUP_W
    return pl.pallas_call(
        _proj_kernel,
        grid=(m // tm, n_tiles),
        in_specs=[
            pl.BlockSpec((tm, D_MODEL), lambda i, j: (i, 0)),
            pl.BlockSpec((1, D_MODEL), lambda i, j: (0, 0)),
            pl.BlockSpec((D_MODEL, GROUP_W), lambda i, j: (0, j)),
            pl.BlockSpec((D_MODEL, GROUP_W), lambda i, j: (0, n_tiles)),
            pl.BlockSpec((D_MODEL, GROUP_W), lambda i, j: (0, n_tiles + 1)),
            pl.BlockSpec((1, HEAD_DIM), lambda i, j: (0, 0)),
            pl.BlockSpec((1, HEAD_DIM), lambda i, j: (0, 0)),
            pl.BlockSpec((tm, HEAD_DIM), lambda i, j: (i, 0)),
            pl.BlockSpec((tm, HEAD_DIM), lambda i, j: (i, 0)),
        ],
        out_specs=[
            pl.BlockSpec((HEADS_PER_GROUP, tm, HEAD_DIM), lambda i, j: (j, i, 0)),
            pl.BlockSpec((tm, GROUP_W), lambda i, j: (i, 0)),
        ],
        out_shape=[
            jax.ShapeDtypeStruct((N_QKV_HEADS, m, HEAD_DIM), F32),
            jax.ShapeDtypeStruct((m, CONV_CH), F32),
        ],
        scratch_shapes=[pltpu.VMEM((tm, D_MODEL), BF16)],
        compiler_params=_cparams(("arbitrary", "arbitrary")),
        name="proj",
    )(x, nw, w_in_bf, w_in_bf, w_in_bf, qn, kn, cos, sin)


def _rope_tables(pos):
    half = HEAD_DIM // 2
    inv = ROPE_THETA ** (-jnp.arange(half, dtype=F32) / half)
    ang = pos.astype(F32)[:, None] * inv[None, :]
    cos = jnp.cos(ang)
    sin = jnp.sin(ang)
    return jnp.concatenate([cos, cos], axis=-1), jnp.concatenate([-sin, sin], axis=-1)


QUERY_BLOCK = 128


def _prompt_attn_kernel(q_ref, k_ref, v_ref, o_ref, lse_ref, kprev, vprev, *, dil):
    i = pl.program_id(0)
    qb = QUERY_BLOCK

    @pl.when(i == 0)
    def _():
        kprev[...] = jnp.zeros_like(kprev)
        vprev[...] = jnp.zeros_like(vprev)

    qi = lax.broadcasted_iota(jnp.int32, (qb, 2 * qb), 0)
    kj = lax.broadcasted_iota(jnp.int32, (qb, 2 * qb), 1)
    valid = (kj >= qi + (qb - N_BACK)) & (kj <= qi + qb) & ((kj >= qb) | (i > 0))
    for h in range(HEADS_PER_GROUP):
        for r in range(dil):
            rows = pl.ds(r, qb, stride=dil) if dil > 1 else pl.ds(0, qb)
            q = q_ref[h, rows, :].astype(BF16)
            k = jnp.concatenate([kprev[h, rows, :], k_ref[h, rows, :]], axis=0).astype(BF16)
            v = jnp.concatenate([vprev[h, rows, :], v_ref[h, rows, :]], axis=0).astype(BF16)
            o, lse = _softmax_pv(jnp.where(valid, _dot_nt(q, k) * SCALE, NEG_INF), v)
            o_ref[h, rows, :] = o
            lse_ref[h, rows, :] = jnp.broadcast_to(lse, (qb, HEAD_DIM))
    kprev[...] = k_ref[...]
    vprev[...] = v_ref[...]


def _prompt_attn(qkv_hm, g):
    s = qkv_hm.shape[1]
    dil = DILATIONS[g]
    sb = QUERY_BLOCK * dil
    blk = (HEADS_PER_GROUP, sb, HEAD_DIM)
    out_spec = pl.BlockSpec(blk, lambda i: (0, i, 0))
    out_shape = jax.ShapeDtypeStruct((HEADS_PER_GROUP, s, HEAD_DIM), F32)
    return pl.pallas_call(
        functools.partial(_prompt_attn_kernel, dil=dil),
        grid=(s // sb,),
        in_specs=[
            pl.BlockSpec(blk, lambda i: (g, i, 0)),
            pl.BlockSpec(blk, lambda i: (N_GROUPS + g, i, 0)),
            pl.BlockSpec(blk, lambda i: (2 * N_GROUPS + g, i, 0)),
        ],
        out_specs=[out_spec, out_spec],
        out_shape=[out_shape, out_shape],
        scratch_shapes=[pltpu.VMEM(blk, F32), pltpu.VMEM(blk, F32)],
        compiler_params=_cparams(("arbitrary",)),
        name=f"prompt_attn_g{g}",
    )(qkv_hm, qkv_hm, qkv_hm)


def _ln_swish(y, lnw, lnb):
    mu = jnp.mean(y, axis=-1, keepdims=True)
    yc = y - mu
    yn = yc * lax.rsqrt(jnp.mean(yc * yc, axis=-1, keepdims=True) + EPS) * lnw + lnb
    return yn * jax.nn.sigmoid(yn)


HALO = 32


def _prompt_conv_kernel(u_ref, halo_ref, cw_ref, cb_ref, lnw_ref, lnb_ref, o_ref, scr):
    i = pl.program_id(0)
    tm = u_ref.shape[0]
    scr[0:HALO, :] = jnp.where(i > 0, halo_ref[...], 0.0)
    scr[HALO:, :] = u_ref[...]
    acc = jnp.zeros((tm, CONV_CH), F32)
    for j in range(CONV_TAPS):
        acc = acc + cw_ref[j:j + 1, :] * scr[pl.ds(j + HALO - (CONV_TAPS - 1), tm), :]
    o_ref[...] = _ln_swish(acc + cb_ref[...], lnw_ref[...], lnb_ref[...])


def _prompt_conv(u, cw, cb, lnw, lnb, tm=512):
    m = u.shape[0]
    vec = pl.BlockSpec((1, CONV_CH), lambda i: (0, 0))
    return pl.pallas_call(
        _prompt_conv_kernel,
        grid=(m // tm,),
        in_specs=[
            pl.BlockSpec((tm, CONV_CH), lambda i: (i, 0)),
            pl.BlockSpec((HALO, CONV_CH), lambda i: (jnp.maximum(i * (tm // HALO) - 1, 0), 0)),
            pl.BlockSpec((CONV_TAPS, CONV_CH), lambda i: (0, 0)),
            vec, vec, vec,
        ],
        out_specs=pl.BlockSpec((tm, CONV_CH), lambda i: (i, 0)),
        out_shape=jax.ShapeDtypeStruct((m, CONV_CH), F32),
        scratch_shapes=[pltpu.VMEM((tm + HALO, CONV_CH), F32)],
        compiler_params=_cparams(("arbitrary",)),
        name="prompt_conv",
    )(u, u, cw, cb, lnw, lnb)


SAMPLE_CONV_ROWS = 40


def _sample_conv_kernel(ucat_ref, wsh_ref, cb_ref, lnw_ref, lnb_ref, o_ref):
    ucat = ucat_ref[...]
    for t in range(DEC_SEQ):
        y = jnp.sum(ucat * wsh_ref[t][None], axis=1)
        o_ref[t] = _ln_swish(y + cb_ref[...], lnw_ref[...], lnb_ref[...])


def _sample_conv(ucat, wsh, cb, lnw, lnb):
    nb = ucat.shape[0]
    return pl.pallas_call(
        _sample_conv_kernel,
        out_shape=jax.ShapeDtypeStruct((DEC_SEQ, nb, CONV_CH), F32),
        compiler_params=pltpu.CompilerParams(vmem_limit_bytes=VMEM_LIMIT),
        name="sample_conv",
    )(ucat, wsh, cb, lnw, lnb)


def _out_proj_kernel(x_ref, o0_ref, o1_ref, o2_ref, l0_ref, l1_ref, l2_ref, cv_ref, w_ref, y_ref, a_scr):
    o_refs = (o0_ref, o1_ref, o2_ref)
    for h in range(HEADS_PER_GROUP):
        l0, l1, l2 = l0_ref[h], l1_ref[h], l2_ref[h]
        lm = jnp.maximum(jnp.maximum(l0, l1), l2)
        es = (jnp.exp(l0 - lm), jnp.exp(l1 - lm), jnp.exp(l2 - lm))
        den = es[0] + es[1] + es[2]
        for g in range(N_GROUPS):
            c0 = g * GROUP_W + h * HEAD_DIM
            a_scr[:, c0:c0 + HEAD_DIM] = ((es[g] / den) * o_refs[g][h]).astype(BF16)
    a_scr[:, ATTN_W:] = cv_ref[...].astype(BF16)
    y_ref[...] = x_ref[...] + _dot(a_scr[...], w_ref[...])


def _out_proj(x, os_, ls_, gidx, conv, w_out_bf, tm):
    m = x.shape[0]
    gspecs = [pl.BlockSpec((HEADS_PER_GROUP, tm, HEAD_DIM), lambda i, gi=gi: (gi, i, 0)) for gi in gidx]
    return pl.pallas_call(
        _out_proj_kernel,
        grid=(m // tm,),
        in_specs=[pl.BlockSpec((tm, D_MODEL), lambda i: (i, 0))] + gspecs + gspecs
        + [pl.BlockSpec((tm, CONV_CH), lambda i: (i, 0)), pl.BlockSpec((D_MODEL, D_MODEL), lambda i: (0, 0))],
        out_specs=pl.BlockSpec((tm, D_MODEL), lambda i: (i, 0)),
        out_shape=jax.ShapeDtypeStruct((m, D_MODEL), F32),
        scratch_shapes=[pltpu.VMEM((tm, D_MODEL), BF16)],
        compiler_params=_cparams(("arbitrary",)),
        name="out_proj",
    )(x, *os_, *ls_, conv, w_out_bf)


def _memory_kv_kernel(mem_ref, nw_ref, wk_ref, wv_ref, kn_ref, k_ref, v_ref):
    h = _rms(mem_ref[...], nw_ref[...]).astype(BF16)
    k = _dot(h, wk_ref[...])
    v_ref[...] = _dot(h, wv_ref[...])
    for hd in range(MEM_W // HEAD_DIM):
        hs = slice(hd * HEAD_DIM, (hd + 1) * HEAD_DIM)
        k_ref[:, hs] = _rms(k[:, hs], kn_ref[...])


def _memory_kv(mem, nw, wk_bf, wv_bf, kn):
    return pl.pallas_call(
        _memory_kv_kernel,
        out_shape=[jax.ShapeDtypeStruct((MEM_TOKENS, MEM_W), F32)] * 2,
        compiler_params=pltpu.CompilerParams(vmem_limit_bytes=VMEM_LIMIT),
        name="memory_kv",
    )(mem, nw, wk_bf, wv_bf, kn)


MEM_HEADS = MEM_W // HEAD_DIM


def _mem_attn_prompt_kernel(x_ref, nw_ref, wq_ref, qn_ref, k_ref, v_ref, wo_ref, y_ref, o_scr):
    x = x_ref[...]
    q = _dot(_rms(x, nw_ref[...]).astype(BF16), wq_ref[...])
    for h in range(MEM_HEADS):
        hs = slice(h * HEAD_DIM, (h + 1) * HEAD_DIM)
        qh = _rms(q[:, hs], qn_ref[...]).astype(BF16)
        s = _dot_nt(qh, k_ref[:, hs].astype(BF16)) * SCALE
        o_scr[:, hs] = _softmax_pv(s, v_ref[:, hs].astype(BF16))[0].astype(BF16)
    y_ref[...] = x + _dot(o_scr[...], wo_ref[...])


def _mem_attn_prompt(x, nw, wq_bf, qn, k, v, wo_bf, tm):
    m = x.shape[0]
    kv_spec = pl.BlockSpec((MEM_TOKENS, MEM_W), lambda i: (0, 0))
    return pl.pallas_call(
        _mem_attn_prompt_kernel,
        grid=(m // tm,),
        in_specs=[
            pl.BlockSpec((tm, D_MODEL), lambda i: (i, 0)),
            pl.BlockSpec((1, D_MODEL), lambda i: (0, 0)),
            pl.BlockSpec((D_MODEL, MEM_W), lambda i: (0, 0)),
            pl.BlockSpec((1, HEAD_DIM), lambda i: (0, 0)),
            kv_spec, kv_spec,
            pl.BlockSpec((MEM_W, D_MODEL), lambda i: (0, 0)),
        ],
        out_specs=pl.BlockSpec((tm, D_MODEL), lambda i: (i, 0)),
        out_shape=jax.ShapeDtypeStruct((m, D_MODEL), F32),
        scratch_shapes=[pltpu.VMEM((tm, MEM_W), BF16)],
        compiler_params=_cparams(("arbitrary",)),
        name="mem_attn_prompt",
    )(x, nw, wq_bf, qn, k, v, wo_bf)


MEM_BB = 8
MEM_ROWS = MEM_BB * DEC_SEQ
MEM_KEYS = MEM_BB * MEM_TOKENS * MEM_HEADS


def _mem_attn_sample_kernel(x_ref, nw_ref, wq_ref, qn_ref, k_ref, v_ref, bias_ref, wo_ref, y_ref, o_scr):
    x = x_ref[...]
    q = _dot(_rms(x, nw_ref[...]).astype(BF16), wq_ref[...])
    qs = jnp.concatenate([_rms(q[:, h * HEAD_DIM:(h + 1) * HEAD_DIM], qn_ref[...]) for h in range(MEM_HEADS)],
                         axis=0).astype(BF16)
    s = _dot_nt(qs, k_ref[...].astype(BF16)) * SCALE + bias_ref[...]
    o = _softmax_pv(s, v_ref[...].astype(BF16))[0]
    for h in range(MEM_HEADS):
        o_scr[:, h * HEAD_DIM:(h + 1) * HEAD_DIM] = o[h * MEM_ROWS:(h + 1) * MEM_ROWS].astype(BF16)
    y_ref[...] = x + _dot(o_scr[...], wo_ref[...])


def _mem_sample_bias():
    r = np.arange(MEM_HEADS * MEM_ROWS)[:, None]
    c = np.arange(MEM_KEYS)[None, :]
    ok = (r // MEM_ROWS == c % MEM_HEADS) & ((r % MEM_ROWS) // DEC_SEQ == c // (MEM_TOKENS * MEM_HEADS))
    return np.where(ok, 0.0, NEG_INF).astype(np.float32)


def _mem_attn_sample(x, nw, wq_bf, qn, k_flat, v_flat, wo_bf):
    m = x.shape[0]
    bias = jnp.asarray(_mem_sample_bias())
    kv_spec = pl.BlockSpec((MEM_KEYS, HEAD_DIM), lambda i: (i, 0))
    return pl.pallas_call(
        _mem_attn_sample_kernel,
        grid=(m // MEM_ROWS,),
        in_specs=[
            pl.BlockSpec((MEM_ROWS, D_MODEL), lambda i: (i, 0)),
            pl.BlockSpec((1, D_MODEL), lambda i: (0, 0)),
            pl.BlockSpec((D_MODEL, MEM_W), lambda i: (0, 0)),
            pl.BlockSpec((1, HEAD_DIM), lambda i: (0, 0)),
            kv_spec, kv_spec,
            pl.BlockSpec(bias.shape, lambda i: (0, 0)),
            pl.BlockSpec((MEM_W, D_MODEL), lambda i: (0, 0)),
        ],
        out_specs=pl.BlockSpec((MEM_ROWS, D_MODEL), lambda i: (i, 0)),
        out_shape=jax.ShapeDtypeStruct((m, D_MODEL), F32),
        scratch_shapes=[pltpu.VMEM((MEM_ROWS, MEM_W), BF16)],
        compiler_params=_cparams(("arbitrary",)),
        name="mem_attn_sample",
    )(x, nw, wq_bf, qn, k_flat, v_flat, bias, wo_bf)


PEER_TQ = 128
NOT_TOP = 99.0
CAND_ROWS = 16 + 7 * 8 + 8
INVALID_FLAT = 999.0


def _cand_flat_index():
    flat = np.full((CAND_ROWS,), INVALID_FLAT, np.float32)
    flat[0:16] = np.arange(16)
    for a in range(1, 8):
        nb = TOPK // (a + 1)
        flat[16 + 8 * (a - 1):16 + 8 * (a - 1) + nb] = a * TOPK + np.arange(nb)
    flat[72:80] = np.arange(8, 16) * TOPK
    return np.broadcast_to(flat[:, None], (CAND_ROWS, PEER_TQ)).copy()


def _topk_rows(sa, sb, row, va_scr, vb_scr, ia_scr, ib_scr):
    for a in range(TOPK):
        ma = jnp.max(sa, axis=0, keepdims=True)
        mb = jnp.max(sb, axis=0, keepdims=True)
        ia = jnp.min(jnp.where(sa == ma, row, float(N_KEYS)), axis=0, keepdims=True)
        ib = jnp.min(jnp.where(sb == mb, row, float(N_KEYS)), axis=0, keepdims=True)
        sa = jnp.where(row == ia, NEG_INF, sa)
        sb = jnp.where(row == ib, NEG_INF, sb)
        va_scr[a:a + 1, :] = ma
        vb_scr[a:a + 1, :] = mb
        ia_scr[a:a + 1, :] = ia
        ib_scr[a:a + 1, :] = ib


def _peer_query_kernel(x_ref, nw_ref, wq_ref, k1_ref, k2_ref, flat_ref, *rest, with_tables):
    if with_tables:
        u_ref, v_ref, *rest = rest
        (h3t_ref, r2_ref, e2_ref, n1_ref, c1_ref, u_bf_ref, vt_bf_ref,
         q_scr, v1_scr, v2_scr, i1_scr, i2_scr) = rest
        u_bf_ref[...] = u_ref[...].astype(BF16)
        vt_bf_ref[...] = v_ref[...].T.astype(BF16)
    else:
        h3t_ref, r2_ref, e2_ref, n1_ref, c1_ref, q_scr, v1_scr, v2_scr, i1_scr, i2_scr = rest
    h3 = _rms(x_ref[...], nw_ref[...])
    h3t_ref[...] = h3.T.astype(BF16)
    q_scr[...] = _dot(h3.astype(BF16), wq_ref[...]).astype(BF16)
    row = lax.broadcasted_iota(jnp.int32, (N_KEYS, PEER_TQ), 0).astype(F32)
    flat = flat_ref[...]
    cand_ok = flat < INVALID_FLAT

    def head(h, carry):
        off = pl.multiple_of(h * 2 * HEAD_DIM, 2 * HEAD_DIM)
        s1 = _dot_nt(k1_ref[h], q_scr[:, pl.ds(off, HEAD_DIM)])
        s2 = _dot_nt(k2_ref[h], q_scr[:, pl.ds(off + HEAD_DIM, HEAD_DIM)])
        _topk_rows(s1, s2, row, v1_scr, v2_scr, i1_scr, i2_scr)
        pieces = [v1_scr[0:1, :] + v2_scr[...]]
        for a in range(1, 8):
            pieces.append(v1_scr[a:a + 1, :] + v2_scr[0:8, :])
        pieces.append(v1_scr[8:16, :] + v2_scr[0:1, :])
        cand0 = jnp.where(cand_ok, jnp.concatenate(pieces, axis=0), NEG_INF)
        cand = cand0
        for _ in range(TOPK):
            m = jnp.max(cand, axis=0, keepdims=True)
            f = jnp.min(jnp.where(cand == m, flat, 2 * INVALID_FLAT), axis=0, keepdims=True)
            cand = jnp.where(flat == f, NEG_INF, cand)
        sel = cand_ok & (cand == NEG_INF)
        z = jnp.sum(jnp.where(sel, jnp.exp(cand0 - cand0[0:1, :]), 0.0), axis=0, keepdims=True)
        self_ = sel.astype(F32)
        n1 = jnp.zeros((N_KEYS, PEER_TQ), F32)
        for a in range(TOPK):
            if a == 0:
                na = jnp.sum(self_[0:16, :], axis=0, keepdims=True)
            elif a < 8:
                na = jnp.sum(self_[16 + 8 * (a - 1):16 + 8 * a, :], axis=0, keepdims=True)
            else:
                na = self_[72 + a - 8:72 + a - 7, :]
            n1 = jnp.where(row == i1_scr[a:a + 1, :], na, n1)
        rank2 = jnp.full((N_KEYS, PEER_TQ), NOT_TOP, F32)
        for a in range(TOPK):
            rank2 = jnp.where(row == i2_scr[a:a + 1, :], float(a), rank2)
        r2_ref[h] = rank2.astype(BF16)
        e2_ref[h] = jnp.exp(s2 - v2_scr[0:1, :]).astype(BF16)
        n1_ref[h] = n1
        c1_ref[h] = jnp.exp(s1 - v1_scr[0:1, :]) / z
        return carry

    lax.fori_loop(0, PEER_HEADS, head, 0)


def _peer_query(x2, nw, wq_bf, k1_bf, k2_bf, tables=None):
    m = x2.shape[0]
    tq = PEER_TQ
    steps = m // tq
    flat = jnp.asarray(_cand_flat_index())
    key_spec = pl.BlockSpec((PEER_HEADS, N_KEYS, HEAD_DIM), lambda i: (0, 0, 0))
    stat_spec = pl.BlockSpec((PEER_HEADS, N_KEYS, tq), lambda i: (0, 0, i))
    stat_shape = lambda dt: jax.ShapeDtypeStruct((PEER_HEADS, N_KEYS, m), dt)
    table_in, table_out, table_shapes = [], [], []
    if tables is not None:
        rows = N_EXPERTS // steps
        table_in = [pl.BlockSpec((rows, D_MODEL), lambda i: (i, 0))] * 2
        table_out = [pl.BlockSpec((rows, D_MODEL), lambda i: (i, 0)), pl.BlockSpec((D_MODEL, rows), lambda i: (0, i))]
        table_shapes = [jax.ShapeDtypeStruct((N_EXPERTS, D_MODEL), BF16), jax.ShapeDtypeStruct((D_MODEL, N_EXPERTS), BF16)]
    return pl.pallas_call(
        functools.partial(_peer_query_kernel, with_tables=tables is not None),
        grid=(steps,),
        in_specs=[
            pl.BlockSpec((tq, D_MODEL), lambda i: (i, 0)),
            pl.BlockSpec((1, D_MODEL), lambda i: (0, 0)),
            pl.BlockSpec((D_MODEL, 2 * HEAD_DIM * PEER_HEADS), lambda i: (0, 0)),
            key_spec, key_spec,
            pl.BlockSpec((CAND_ROWS, tq), lambda i: (0, 0)),
        ] + table_in,
        out_specs=[pl.BlockSpec((D_MODEL, tq), lambda i: (0, i))] + [stat_spec] * 4 + table_out,
        out_shape=[jax.ShapeDtypeStruct((D_MODEL, m), BF16)] + [stat_shape(BF16)] * 2 + [stat_shape(F32)] * 2
        + table_shapes,
        scratch_shapes=[
            pltpu.VMEM((tq, 2 * HEAD_DIM * PEER_HEADS), BF16),
            pltpu.VMEM((TOPK, tq), F32),
            pltpu.VMEM((TOPK, tq), F32),
            pltpu.VMEM((TOPK, tq), F32),
            pltpu.VMEM((TOPK, tq), F32),
        ],
        compiler_params=_cparams(("arbitrary",)),
        name="peer_query",
    )(x2, nw, wq_bf, k1_bf, k2_bf, flat, *(tables or ()))


PEER_TM = 512
PEER_TE = 512
KEYS_PER_TILE = PEER_TE // N_KEYS
N_TILES = N_EXPERTS // PEER_TE
LANE = 128
BF16_ROWS = 16
INV_SQRT2 = 0.7071067811865476


def _gate_lanes(act_ref, wt_ref, r2_ref, e2_ref, n1_ref, c1_ref, key0, lc):
    ls = slice(lc * LANE, (lc + 1) * LANE)
    shape3 = (N_KEYS // BF16_ROWS, BF16_ROWS, LANE)
    gates = [jnp.zeros(shape3, BF16) for _ in range(KEYS_PER_TILE)]
    for h in range(PEER_HEADS):
        r2 = r2_ref[h, :, :, ls]
        e2 = e2_ref[h, :, :, ls]
        for c in range(KEYS_PER_TILE):
            n1 = jnp.broadcast_to(n1_ref[h, key0 + c:key0 + c + 1, ls], shape3[1:]).astype(BF16)
            c1 = jnp.broadcast_to(c1_ref[h, key0 + c:key0 + c + 1, ls], shape3[1:]).astype(BF16)
            gates[c] = gates[c] + (jnp.clip(n1 - r2, 0.0, 1.0) * e2) * c1
    for c in range(KEYS_PER_TILE):
        a = act_ref[c * N_KEYS:(c + 1) * N_KEYS, ls]
        gelu = 0.5 * a * (1.0 + lax.erf(a * INV_SQRT2))
        wt_ref[c * N_KEYS:(c + 1) * N_KEYS, ls] = (gates[c] * gelu.astype(BF16).reshape(shape3)).reshape(N_KEYS, LANE)


ACT_HALF = PEER_TM // 2
OUT_ROWS = D_MODEL // 4
assert D_MODEL // OUT_ROWS == PEER_TM // LANE


def _pipeline_stages(h3t_ref, acc_scr, stats, *, u_ref, act_out, act_in, key0, wt_out, wt_in, vt_ref):
    def act_piece(n):
        ls = slice(n * ACT_HALF, (n + 1) * ACT_HALF)
        act_out[:, ls] = _dot(u_ref[...], h3t_ref[:, ls])

    def out_piece(r):
        rs = slice(r * OUT_ROWS, (r + 1) * OUT_ROWS)
        acc_scr[rs, :] += _dot(vt_ref[rs, :], wt_in[...])

    def gate_piece(lc):
        _gate_lanes(act_in, wt_out, *stats, key0, lc)

    for r in range(D_MODEL // OUT_ROWS):
        out_piece(r)
        gate_piece(r)
    act_piece(0)
    act_piece(1)


def _peer_dense_kernel(h3t_ref, u0_ref, u_ref, vt_ref, vtl_ref, r2_ref, e2_ref, n1_ref, c1_ref,
                       x2_ref, y_ref, acc_scr, act0, act1, wt0, wt1):
    j = pl.program_id(1)
    stats = (r2_ref, e2_ref, n1_ref, c1_ref)

    @pl.when(j == 0)
    def _():
        acc_scr[...] = jnp.zeros_like(acc_scr)
        wt1[...] = jnp.zeros_like(wt1)
        act0[...] = _dot(u0_ref[...], h3t_ref[...])

    @pl.when(j % 2 == 0)
    def _():
        _pipeline_stages(h3t_ref, acc_scr, stats, u_ref=u_ref, act_out=act1, act_in=act0, key0=0,
                         wt_out=wt0, wt_in=wt1, vt_ref=vt_ref)

    @pl.when(j % 2 == 1)
    def _():
        _pipeline_stages(h3t_ref, acc_scr, stats, u_ref=u_ref, act_out=act0, act_in=act1, key0=KEYS_PER_TILE,
                         wt_out=wt1, wt_in=wt0, vt_ref=vt_ref)

    @pl.when(j == pl.num_programs(1) - 1)
    def _():
        y_ref[...] = x2_ref[...] + (acc_scr[...] + _dot(vtl_ref[...], wt1[...])).T


def _peer_dense(h3t, u_bf, vt_bf, r2, e2, n1, c1, x2):
    m = x2.shape[0]
    tm, te = PEER_TM, PEER_TE
    last = N_TILES - 1
    once = pl.Buffered(1)
    key2_shape = (PEER_HEADS, N_KEYS // BF16_ROWS, BF16_ROWS, m)
    r2, e2 = r2.reshape(key2_shape), e2.reshape(key2_shape)
    key2_spec = pl.BlockSpec(key2_shape[:3] + (tm,), lambda i, j: (0, 0, 0, i), pipeline_mode=once)
    key1_spec = pl.BlockSpec((PEER_HEADS, 2 * KEYS_PER_TILE, tm), lambda i, j: (0, j // 2, i))
    u_blk, vt_blk = (te, D_MODEL), (D_MODEL, te)
    return pl.pallas_call(
        _peer_dense_kernel,
        grid=(m // tm, N_TILES),
        in_specs=[
            pl.BlockSpec((D_MODEL, tm), lambda i, j: (0, i)),
            pl.BlockSpec(u_blk, lambda i, j: (0, 0), pipeline_mode=once),
            pl.BlockSpec(u_blk, lambda i, j: (jnp.minimum(j + 1, last), 0)),
            pl.BlockSpec(vt_blk, lambda i, j: (0, jnp.maximum(j - 1, 0))),
            pl.BlockSpec(vt_blk, lambda i, j: (0, last), pipeline_mode=once),
            key2_spec, key2_spec, key1_spec, key1_spec,
            pl.BlockSpec((tm, D_MODEL), lambda i, j: (i, 0), pipeline_mode=once),
        ],
        out_specs=pl.BlockSpec((tm, D_MODEL), lambda i, j: (i, 0)),
        out_shape=jax.ShapeDtypeStruct((m, D_MODEL), F32),
        scratch_shapes=[
            pltpu.VMEM((D_MODEL, tm), F32),
            pltpu.VMEM((te, tm), F32),
            pltpu.VMEM((te, tm), F32),
            pltpu.VMEM((te, tm), BF16),
            pltpu.VMEM((te, tm), BF16),
        ],
        compiler_params=_cparams(("arbitrary", "arbitrary")),
        name="peer_dense",
    )(h3t, u_bf, u_bf, vt_bf, vt_bf, r2, e2, n1, c1, x2)


def _peer(x2, nw, wq_bf, k1_bf, k2_bf, u_bf=None, vt_bf=None, tables=None):
    h3t, r2, e2, n1, c1, *made = _peer_query(x2, nw, wq_bf, k1_bf, k2_bf, tables)
    if tables is not None:
        u_bf, vt_bf = made
    return _peer_dense(h3t, u_bf, vt_bf, r2, e2, n1, c1, x2), u_bf, vt_bf


SAMPLE_BB = 1
SAMPLE_ROWS = SAMPLE_BB * DEC_SEQ
SAMPLE_Q = HEADS_PER_GROUP * SAMPLE_ROWS
NEW_PAD = 128
NEW_ROWS = DEC_SEQ * HEADS_PER_GROUP
SAMPLE_NKC = tuple(SAMPLE_BB * HEADS_PER_GROUP * n for n in (WINDOWS[0], WINDOWS[1], DEC_SEQ * N_BACK))
ROWS_PER_CHUNK = DILATIONS[2] * HEADS_PER_GROUP


def _sample_bias(g):
    nkc = SAMPLE_NKC[g]
    r = np.arange(SAMPLE_Q)[:, None]
    rh, rb, rt = r // SAMPLE_ROWS, (r % SAMPLE_ROWS) // DEC_SEQ, r % DEC_SEQ
    c = np.arange(nkc)[None, :]
    per_batch = nkc // SAMPLE_BB
    cb, ch = c // per_batch, c % HEADS_PER_GROUP
    if g == 0:
        ok_pos = (c % per_batch) // HEADS_PER_GROUP >= rt
    elif g == 1:
        ok_pos = ((c % per_batch) // HEADS_PER_GROUP) % DILATIONS[1] == rt
    else:
        ok_pos = (c % (DEC_SEQ * HEADS_PER_GROUP)) // HEADS_PER_GROUP == rt
    ok_c = (cb == rb) & (ch == rh) & ok_pos
    n = np.arange(NEW_PAD)[None, :]
    nh, nb, nt = n // SAMPLE_ROWS, (n % SAMPLE_ROWS) // DEC_SEQ, n % DEC_SEQ
    ok_n = (n < SAMPLE_Q) & (nh == rh) & (nb == rb) & ((nt <= rt) if g == 0 else (nt == rt))
    return np.where(np.concatenate([ok_c, ok_n], axis=1), 0.0, NEG_INF).astype(np.float32)


def _sample_attn_kernel(qkv_ref, new_ref, k0_ref, v0_ref, k1_ref, v1_ref, k2_ref, v2_ref, b0_ref, b1_ref, b2_ref,
                        o_ref, lse_ref, ok0_ref, ov0_ref, ok1_ref, ov1_ref, ok2_ref, ov2_ref):
    olds = (k0_ref, v0_ref, k1_ref, v1_ref, k2_ref, v2_ref)
    outs = (ok0_ref, ov0_ref, ok1_ref, ov1_ref, ok2_ref, ov2_ref)
    biases = (b0_ref, b1_ref, b2_ref)

    for c in range(2 * N_GROUPS):
        new = new_ref[0, c * NEW_ROWS:(c + 1) * NEW_ROWS, :]
        if c < 4:
            keep = olds[c].shape[1] - NEW_ROWS
            outs[c][0, 0:keep, :] = olds[c][0, NEW_ROWS:, :]
            outs[c][0, keep:, :] = new
        else:
            keep = ROWS_PER_CHUNK - NEW_ROWS
            last = olds[c].shape[0] - 1
            outs[c][:, 0:keep, :] = olds[c][:, NEW_ROWS:, :]
            outs[c][0:last, keep:, :] = olds[c][1:, 0:NEW_ROWS, :]
            outs[c][last, keep:, :] = new

    pad = jnp.zeros((NEW_PAD - SAMPLE_Q, HEAD_DIM), F32)
    for g in range(N_GROUPS):
        rows = slice(g * SAMPLE_Q, (g + 1) * SAMPLE_Q)
        q, kn, vn = (qkv_ref[0, w * N_GROUPS * SAMPLE_Q + g * SAMPLE_Q:w * N_GROUPS * SAMPLE_Q + (g + 1) * SAMPLE_Q, :]
                     for w in range(3))
        if g < 2:
            kc, vc = olds[2 * g][0], olds[2 * g + 1][0]
        else:
            kc = olds[4][:, 0:NEW_ROWS, :].reshape(SAMPLE_NKC[2], HEAD_DIM)
            vc = olds[5][:, 0:NEW_ROWS, :].reshape(SAMPLE_NKC[2], HEAD_DIM)
        k = jnp.concatenate([kc, kn, pad], axis=0).astype(BF16)
        v = jnp.concatenate([vc, vn, pad], axis=0).astype(BF16)
        s = _dot_nt(q.astype(BF16), k) * SCALE + biases[g][...]
        o, lse = _softmax_pv(s, v)
        o_ref[0, rows, :] = o
        lse_ref[0, rows, :] = jnp.broadcast_to(lse, (SAMPLE_Q, HEAD_DIM))


def _sample_attn(qkv_hm, caches):
    nbatch = qkv_hm.shape[1] // DEC_SEQ
    per_batch = qkv_hm.reshape(N_QKV_HEADS, nbatch, DEC_SEQ, HEAD_DIM).transpose(1, 0, 2, 3)
    new = per_batch[:, N_ATTN_HEADS:].reshape(nbatch, 2, N_GROUPS, HEADS_PER_GROUP, DEC_SEQ, HEAD_DIM)
    new = new.transpose(0, 2, 1, 4, 3, 5).reshape(nbatch, 2 * N_GROUPS * NEW_ROWS, HEAD_DIM)
    args = [per_batch.reshape(nbatch, N_QKV_HEADS * DEC_SEQ, HEAD_DIM), new]
    specs = [pl.BlockSpec((1,) + a.shape[1:], lambda i: (i, 0, 0)) for a in args]
    cache_specs, cache_shapes = [], []
    for g in range(N_GROUPS):
        for cache in caches[2 * g:2 * g + 2]:
            rows = cache.shape[1] * HEADS_PER_GROUP
            if g < 2:
                view = cache.reshape(nbatch, rows, HEAD_DIM)
                cache_specs.append(pl.BlockSpec((1, rows, HEAD_DIM), lambda i: (i, 0, 0)))
            else:
                view = cache.reshape(nbatch * rows // ROWS_PER_CHUNK, ROWS_PER_CHUNK, HEAD_DIM)
                cache_specs.append(pl.BlockSpec((rows // ROWS_PER_CHUNK, ROWS_PER_CHUNK, HEAD_DIM), lambda i: (i, 0, 0)))
            args.append(view)
            cache_shapes.append(jax.ShapeDtypeStruct(view.shape, F32))
    specs += cache_specs
    for g in range(N_GROUPS):
        bias = jnp.asarray(_sample_bias(g))
        args.append(bias)
        specs.append(pl.BlockSpec(bias.shape, lambda i: (0, 0)))
    out_spec = pl.BlockSpec((1, N_GROUPS * SAMPLE_Q, HEAD_DIM), lambda i: (i, 0, 0))
    out_shape = jax.ShapeDtypeStruct((nbatch, N_GROUPS * SAMPLE_Q, HEAD_DIM), F32)
    o, lse, *slid = pl.pallas_call(
        _sample_attn_kernel,
        grid=(nbatch,),
        in_specs=specs,
        out_specs=[out_spec, out_spec] + cache_specs,
        out_shape=[out_shape, out_shape] + cache_shapes,
        compiler_params=_cparams(("arbitrary",)),
        name="sample_attn",
    )(*args)
    head_major = lambda a: a.reshape(nbatch, N_ATTN_HEADS, DEC_SEQ, HEAD_DIM).transpose(1, 0, 2, 3).reshape(
        N_ATTN_HEADS, nbatch * DEC_SEQ, HEAD_DIM)
    return head_major(o), head_major(lse), [s_.reshape(c.shape) for s_, c in zip(slid, caches)]


def kernel(x_prompt, x_sample, mem_prompt, cache_k_w128, cache_v_w128, cache_k_w512, cache_v_w512, cache_k_w2048, cache_v_w2048, state_conv, cache_mem_k, cache_mem_v, norm_mix_w, w_in, q_norm_w, k_norm_w, conv_w, conv_b, conv_ln_w, conv_ln_b, w_out, norm_mem_w, mem_norm_w, wq_mem, wk_mem, wv_mem, qn_mem_w, kn_mem_w, wo_mem, norm_ffn_w, w_peer_q, peer_keys1, peer_keys2, peer_u, peer_v):
    s = x_prompt.shape[1]
    nbatch, t_new = x_sample.shape[:2]
    ms = nbatch * t_new

    row = lambda w: w.reshape(1, -1)
    w_in_bf = w_in.astype(BF16)
    w_out_bf = w_out.astype(BF16)
    wq_mem_bf, wk_mem_bf, wv_mem_bf, wo_mem_bf = (w.astype(BF16) for w in (wq_mem, wk_mem, wv_mem, wo_mem))
    wq_peer_bf = w_peer_q.astype(BF16)
    k1_bf = peer_keys1.astype(BF16)
    k2_bf = peer_keys2.astype(BF16)
    peer_w = (row(norm_ffn_w), wq_peer_bf, k1_bf, k2_bf)
    conv_vecs = (row(conv_b), row(conv_ln_w), row(conv_ln_b))

    xp = x_prompt.reshape(s, D_MODEL)
    cos_p, sin_p = _rope_tables(jnp.arange(s, dtype=jnp.int32))
    qkv_p, u_p = _project(xp, row(norm_mix_w), w_in_bf, row(q_norm_w), row(k_norm_w), cos_p, sin_p, tm=1024)
    attn = [_prompt_attn(qkv_p, g) for g in range(N_GROUPS)]
    conv_p = _prompt_conv(u_p, conv_w, *conv_vecs)
    x1_p = _out_proj(xp, [a[0] for a in attn], [a[1] for a in attn], (0, 0, 0), conv_p, w_out_bf, tm=512)
    mem_k, mem_v = _memory_kv(mem_prompt.reshape(MEM_TOKENS, D_MODEL), row(mem_norm_w), wk_mem_bf, wv_mem_bf,
                              row(kn_mem_w))
    x2_p = _mem_attn_prompt(x1_p, row(norm_mem_w), wq_mem_bf, row(qn_mem_w), mem_k, mem_v, wo_mem_bf, tm=512)
    y_p, u_bf, vt_bf = _peer(x2_p, *peer_w, tables=(peer_u, peer_v))

    def group_heads(qkv_hm, which, g):
        h0 = which * N_ATTN_HEADS + g * HEADS_PER_GROUP
        return qkv_hm[h0:h0 + HEADS_PER_GROUP]

    p_win = []
    for g, win in enumerate(WINDOWS):
        keep = min(win, s)
        for which in (1, 2):
            heads = group_heads(qkv_p, which, g)[:, s - keep:]
            p_win.append(heads.transpose(1, 0, 2)[None])
    p_conv = u_p[s - (CONV_TAPS - 1):].reshape(1, CONV_TAPS - 1, CONV_CH)
    kv4 = lambda a: a.reshape(1, MEM_TOKENS, MEM_HEADS, HEAD_DIM)

    xs = x_sample.reshape(ms, D_MODEL)
    cos_s, sin_s = _rope_tables(PAST_LEN + jnp.arange(ms, dtype=jnp.int32) % t_new)
    qkv_s, u_s = _project(xs, row(norm_mix_w), w_in_bf, row(q_norm_w), row(k_norm_w), cos_s, sin_s, tm=ms)
    caches = (cache_k_w128, cache_v_w128, cache_k_w512, cache_v_w512, cache_k_w2048, cache_v_w2048)
    o_s, lse_s, s_win = _sample_attn(qkv_s, caches)

    n_state = CONV_TAPS - 1
    ucat = jnp.concatenate(
        [state_conv, u_s.reshape(nbatch, t_new, CONV_CH),
         jnp.zeros((nbatch, SAMPLE_CONV_ROWS - n_state - t_new, CONV_CH), F32)], axis=1)
    wsh = jnp.stack([jnp.pad(conv_w, ((t, SAMPLE_CONV_ROWS - CONV_TAPS - t), (0, 0))) for t in range(t_new)])
    conv_s = _sample_conv(ucat, wsh, *conv_vecs)
    conv_s = conv_s.transpose(1, 0, 2).reshape(ms, CONV_CH)
    s_conv = ucat[:, t_new:t_new + n_state]
    x1_s = _out_proj(xs, [o_s] * N_GROUPS, [lse_s] * N_GROUPS, (0, 1, 2), conv_s, w_out_bf, tm=ms)
    x2_s = _mem_attn_sample(x1_s, row(norm_mem_w), wq_mem_bf, row(qn_mem_w),
                            cache_mem_k.reshape(-1, HEAD_DIM), cache_mem_v.reshape(-1, HEAD_DIM), wo_mem_bf)
    y_s = _peer(x2_s, *peer_w, u_bf, vt_bf)[0]

    return (y_p.reshape(x_prompt.shape), y_s.reshape(x_sample.shape), *p_win, p_conv, kv4(mem_k), kv4(mem_v),
            *s_win, s_conv)
```

```python
import functools

import numpy as np
import jax
import jax.numpy as jnp
from jax import lax
from jax.experimental import pallas as pl
from jax.experimental.pallas import tpu as pltpu

F32 = jnp.float32
BF16 = jnp.bfloat16

D_MODEL = 2048
HEAD_DIM = 128
HEADS_PER_GROUP = 4
GROUP_W = HEADS_PER_GROUP * HEAD_DIM
DILATIONS = (1, 4, 16)
WINDOWS = (128, 512, 2048)
N_BACK = 128
N_GROUPS = 3
ATTN_W = N_GROUPS * GROUP_W
CONV_CH = 512
CONV_TAPS = 31
QKV_W = 3 * ATTN_W
N_ATTN_HEADS = N_GROUPS * HEADS_PER_GROUP
N_QKV_HEADS = 3 * N_ATTN_HEADS
MEM_TOKENS = 256
MEM_W = 512
PEER_HEADS = 8
N_KEYS = 128
N_EXPERTS = N_KEYS * N_KEYS
TOPK = 16
PAST_LEN = 2048
DEC_SEQ = 4
ROPE_THETA = 10000.0
EPS = 1e-6
SCALE = HEAD_DIM ** -0.5
NEG_INF = float("-inf")

VMEM_LIMIT = 56 * 1024 * 1024


def _cparams(sem):
    return pltpu.CompilerParams(dimension_semantics=sem, vmem_limit_bytes=VMEM_LIMIT)


def _rms(x, w):
    return x * lax.rsqrt(jnp.mean(x * x, axis=-1, keepdims=True) + EPS) * w


def _dot(a, b):
    return jnp.dot(a, b, preferred_element_type=F32)


def _dot_nt(a, b):
    return lax.dot_general(a, b, (((1,), (1,)), ((), ())), preferred_element_type=F32)


def _softmax_pv(s, v):
    m = jnp.max(s, axis=-1, keepdims=True)
    p = jnp.exp(s - m)
    den = jnp.sum(p, axis=-1, keepdims=True)
    return _dot(p.astype(BF16), v) / den, m + jnp.log(den)


def _proj_kernel(x_ref, nw_ref, w_ref, wa_ref, wg_ref, qn_ref, kn_ref, cos_ref, sin_ref,
                 qkv_ref, u_ref, h_scr):
    j = pl.program_id(1)

    @pl.when(j == 0)
    def _():
        h_scr[...] = _rms(x_ref[...], nw_ref[...]).astype(BF16)
        a = _dot(h_scr[...], wa_ref[...])
        g = _dot(h_scr[...], wg_ref[...])
        u_ref[...] = a * jax.nn.sigmoid(g)

    z = _dot(h_scr[...], w_ref[...])

    @pl.when(j < 2 * N_GROUPS)
    def _():
        w = jnp.where(j < N_GROUPS, qn_ref[...], kn_ref[...])
        cos = cos_ref[...]
        sin = sin_ref[...]
        for h in range(HEADS_PER_GROUP):
            y = _rms(z[:, h * HEAD_DIM:(h + 1) * HEAD_DIM], w)
            qkv_ref[h] = y * cos + pltpu.roll(y, HEAD_DIM // 2, 1) * sin

    @pl.when(j >= 2 * N_GROUPS)
    def _():
        for h in range(HEADS_PER_GROUP):
            qkv_ref[h] = z[:, h * HEAD_DIM:(h + 1) * HEAD_DIM]


def _project(x, nw, w_in_bf, qn, kn, cos, sin, tm):
    m = x.shape[0]
    n_tiles = QKV_W // GROUP_W
    return pl.pallas_call(
        _proj_kernel,
        grid=(m // tm, n_tiles),
        in_specs=[
            pl.BlockSpec((tm, D_MODEL), lambda i, j: (i, 0)),
            pl.BlockSpec((1, D_MODEL), lambda i, j: (0, 0)),
            pl.BlockSpec((D_MODEL, GROUP_W), lambda i, j: (0, j)),
            pl.BlockSpec((D_MODEL, GROUP_W), lambda i, j: (0, n_tiles)),
            pl.BlockSpec((D_MODEL, GROUP_W), lambda i, j: (0, n_tiles + 1)),
            pl.BlockSpec((1, HEAD_DIM), lambda i, j: (0, 0)),
            pl.BlockSpec((1, HEAD_DIM), lambda i, j: (0, 0)),
            pl.BlockSpec((tm, HEAD_DIM), lambda i, j: (i, 0)),
            pl.BlockSpec((tm, HEAD_DIM), lambda i, j: (i, 0)),
        ],
        out_specs=[
            pl.BlockSpec((HEADS_PER_GROUP, tm, HEAD_DIM), lambda i, j: (j, i, 0)),
            pl.BlockSpec((tm, GROUP_W), lambda i, j: (i, 0)),
        ],
        out_shape=[
            jax.ShapeDtypeStruct((N_QKV_HEADS, m, HEAD_DIM), F32),
            jax.ShapeDtypeStruct((m, CONV_CH), F32),
        ],
        scratch_shapes=[pltpu.VMEM((tm, D_MODEL), BF16)],
        compiler_params=_cparams(("arbitrary", "arbitrary")),
        name="proj",
    )(x, nw, w_in_bf, w_in_bf, w_in_bf, qn, kn, cos, sin)


def _rope_tables(pos):
    half = HEAD_DIM // 2
    inv = ROPE_THETA ** (-jnp.arange(half, dtype=F32) / half)
    ang = pos.astype(F32)[:, None] * inv[None, :]
    cos = jnp.cos(ang)
    sin = jnp.sin(ang)
    return jnp.concatenate([cos, cos], axis=-1), jnp.concatenate([-sin, sin], axis=-1)


QUERY_BLOCK = 128


def _prompt_attn_kernel(q_ref, k_ref, v_ref, o_ref, lse_ref, kprev, vprev, *, dil):
    i = pl.program_id(0)
    qb = QUERY_BLOCK

    @pl.when(i == 0)
    def _():
        kprev[...] = jnp.zeros_like(kprev)
        vprev[...] = jnp.zeros_like(vprev)

    qi = lax.broadcasted_iota(jnp.int32, (qb, 2 * qb), 0)
    kj = lax.broadcasted_iota(jnp.int32, (qb, 2 * qb), 1)
    valid = (kj >= qi + (qb - N_BACK)) & (kj <= qi + qb) & ((kj >= qb) | (i > 0))
    for h in range(HEADS_PER_GROUP):
        for r in range(dil):
            rows = pl.ds(r, qb, stride=dil) if dil > 1 else pl.ds(0, qb)
            q = q_ref[h, rows, :].astype(BF16)
            k = jnp.concatenate([kprev[h, rows, :], k_ref[h, rows, :]], axis=0).astype(BF16)
            v = jnp.concatenate([vprev[h, rows, :], v_ref[h, rows, :]], axis=0).astype(BF16)
            o, lse = _softmax_pv(jnp.where(valid, _dot_nt(q, k) * SCALE, NEG_INF), v)
            o_ref[h, rows, :] = o
            lse_ref[h, rows, :] = jnp.broadcast_to(lse, (qb, HEAD_DIM))
    kprev[...] = k_ref[...]
    vprev[...] = v_ref[...]


def _prompt_attn(qkv_hm, g):
    s = qkv_hm.shape[1]
    dil = DILATIONS[g]
    sb = QUERY_BLOCK * dil
    blk = (HEADS_PER_GROUP, sb, HEAD_DIM)
    out_spec = pl.BlockSpec(blk, lambda i: (0, i, 0))
    out_shape = jax.ShapeDtypeStruct((HEADS_PER_GROUP, s, HEAD_DIM), F32)
    return pl.pallas_call(
        functools.partial(_prompt_attn_kernel, dil=dil),
        grid=(s // sb,),
        in_specs=[
            pl.BlockSpec(blk, lambda i: (g, i, 0)),
            pl.BlockSpec(blk, lambda i: (N_GROUPS + g, i, 0)),
            pl.BlockSpec(blk, lambda i: (2 * N_GROUPS + g, i, 0)),
        ],
        out_specs=[out_spec, out_spec],
        out_shape=[out_shape, out_shape],
        scratch_shapes=[pltpu.VMEM(blk, F32), pltpu.VMEM(blk, F32)],
        compiler_params=_cparams(("arbitrary",)),
        name=f"prompt_attn_g{g}",
    )(qkv_hm, qkv_hm, qkv_hm)


def _ln_swish(y, lnw, lnb):
    mu = jnp.mean(y, axis=-1, keepdims=True)
    yc = y - mu
    yn = yc * lax.rsqrt(jnp.mean(yc * yc, axis=-1, keepdims=True) + EPS) * lnw + lnb
    return yn * jax.nn.sigmoid(yn)


HALO = 32


def _prompt_conv_kernel(u_ref, halo_ref, cw_ref, cb_ref, lnw_ref, lnb_ref, o_ref, scr):
    i = pl.program_id(0)
    tm = u_ref.shape[0]
    scr[0:HALO, :] = jnp.where(i > 0, halo_ref[...], 0.0)
    scr[HALO:, :] = u_ref[...]
    acc = jnp.zeros((tm, CONV_CH), F32)
    for j in range(CONV_TAPS):
        acc = acc + cw_ref[j:j + 1, :] * scr[pl.ds(j + HALO - (CONV_TAPS - 1), tm), :]
    o_ref[...] = _ln_swish(acc + cb_ref[...], lnw_ref[...], lnb_ref[...])


def _prompt_conv(u, cw, cb, lnw, lnb, tm=512):
    m = u.shape[0]
    vec = pl.BlockSpec((1, CONV_CH), lambda i: (0, 0))
    return pl.pallas_call(
        _prompt_conv_kernel,
        grid=(m // tm,),
        in_specs=[
            pl.BlockSpec((tm, CONV_CH), lambda i: (i, 0)),
            pl.BlockSpec((HALO, CONV_CH), lambda i: (jnp.maximum(i * (tm // HALO) - 1, 0), 0)),
            pl.BlockSpec((CONV_TAPS, CONV_CH), lambda i: (0, 0)),
            vec, vec, vec,
        ],
        out_specs=pl.BlockSpec((tm, CONV_CH), lambda i: (i, 0)),
        out_shape=jax.ShapeDtypeStruct((m, CONV_CH), F32),
        scratch_shapes=[pltpu.VMEM((tm + HALO, CONV_CH), F32)],
        compiler_params=_cparams(("arbitrary",)),
        name="prompt_conv",
    )(u, u, cw, cb, lnw, lnb)


SAMPLE_CONV_ROWS = 40


def _sample_conv_kernel(ucat_ref, wsh_ref, cb_ref, lnw_ref, lnb_ref, o_ref):
    ucat = ucat_ref[...]
    for t in range(DEC_SEQ):
        y = jnp.sum(ucat * wsh_ref[t][None], axis=1)
        o_ref[t] = _ln_swish(y + cb_ref[...], lnw_ref[...], lnb_ref[...])


def _sample_conv(ucat, wsh, cb, lnw, lnb):
    nb = ucat.shape[0]
    return pl.pallas_call(
        _sample_conv_kernel,
        out_shape=jax.ShapeDtypeStruct((DEC_SEQ, nb, CONV_CH), F32),
        compiler_params=pltpu.CompilerParams(vmem_limit_bytes=VMEM_LIMIT),
        name="sample_conv",
    )(ucat, wsh, cb, lnw, lnb)


def _combine_and_project(x_ref, o_refs, l_refs, cv_ref, w_ref, a_scr):
    for h in range(HEADS_PER_GROUP):
        l0, l1, l2 = (l_ref[h] for l_ref in l_refs)
        lm = jnp.maximum(jnp.maximum(l0, l1), l2)
        es = (jnp.exp(l0 - lm), jnp.exp(l1 - lm), jnp.exp(l2 - lm))
        den = es[0] + es[1] + es[2]
        for g in range(N_GROUPS):
            c0 = g * GROUP_W + h * HEAD_DIM
            a_scr[:, c0:c0 + HEAD_DIM] = ((es[g] / den) * o_refs[g][h]).astype(BF16)
    a_scr[:, ATTN_W:] = cv_ref[...].astype(BF16)
    return x_ref[...] + _dot(a_scr[...], w_ref[...])


def _out_proj_kernel(x_ref, o0_ref, o1_ref, o2_ref, l0_ref, l1_ref, l2_ref, cv_ref, w_ref, y_ref, a_scr):
    y_ref[...] = _combine_and_project(x_ref, (o0_ref, o1_ref, o2_ref), (l0_ref, l1_ref, l2_ref), cv_ref, w_ref, a_scr)


def _out_proj(x, os_, ls_, gidx, conv, w_out_bf, tm):
    m = x.shape[0]
    gspecs = [pl.BlockSpec((HEADS_PER_GROUP, tm, HEAD_DIM), lambda i, gi=gi: (gi, i, 0)) for gi in gidx]
    return pl.pallas_call(
        _out_proj_kernel,
        grid=(m // tm,),
        in_specs=[pl.BlockSpec((tm, D_MODEL), lambda i: (i, 0))] + gspecs + gspecs
        + [pl.BlockSpec((tm, CONV_CH), lambda i: (i, 0)), pl.BlockSpec((D_MODEL, D_MODEL), lambda i: (0, 0))],
        out_specs=pl.BlockSpec((tm, D_MODEL), lambda i: (i, 0)),
        out_shape=jax.ShapeDtypeStruct((m, D_MODEL), F32),
        scratch_shapes=[pltpu.VMEM((tm, D_MODEL), BF16)],
        compiler_params=_cparams(("arbitrary",)),
        name="out_proj",
    )(x, *os_, *ls_, conv, w_out_bf)


def _memory_kv_kernel(mem_ref, nw_ref, wk_ref, wv_ref, kn_ref, k_ref, v_ref):
    h = _rms(mem_ref[...], nw_ref[...]).astype(BF16)
    k = _dot(h, wk_ref[...])
    v_ref[...] = _dot(h, wv_ref[...])
    for hd in range(MEM_W // HEAD_DIM):
        hs = slice(hd * HEAD_DIM, (hd + 1) * HEAD_DIM)
        k_ref[:, hs] = _rms(k[:, hs], kn_ref[...])


def _memory_kv(mem, nw, wk_bf, wv_bf, kn):
    return pl.pallas_call(
        _memory_kv_kernel,
        out_shape=[jax.ShapeDtypeStruct((MEM_TOKENS, MEM_W), F32)] * 2,
        compiler_params=pltpu.CompilerParams(vmem_limit_bytes=VMEM_LIMIT),
        name="memory_kv",
    )(mem, nw, wk_bf, wv_bf, kn)


MEM_HEADS = MEM_W // HEAD_DIM


def _mix_mem_prompt_kernel(x_ref, o0_ref, o1_ref, o2_ref, l0_ref, l1_ref, l2_ref, cv_ref, w_ref,
                           nw_ref, wq_ref, qn_ref, k_ref, v_ref, wo_ref, y_ref, a_scr, o_scr):
    x1 = _combine_and_project(x_ref, (o0_ref, o1_ref, o2_ref), (l0_ref, l1_ref, l2_ref), cv_ref, w_ref, a_scr)
    q = _dot(_rms(x1, nw_ref[...]).astype(BF16), wq_ref[...])
    for h in range(MEM_HEADS):
        hs = slice(h * HEAD_DIM, (h + 1) * HEAD_DIM)
        qh = _rms(q[:, hs], qn_ref[...]).astype(BF16)
        s = _dot_nt(qh, k_ref[:, hs].astype(BF16)) * SCALE
        o_scr[:, hs] = _softmax_pv(s, v_ref[:, hs].astype(BF16))[0].astype(BF16)
    y_ref[...] = x1 + _dot(o_scr[...], wo_ref[...])


def _mix_mem_prompt(x, os_, ls_, conv, w_out_bf, nw, wq_bf, qn, k, v, wo_bf, tm):
    m = x.shape[0]
    once = pl.Buffered(1)
    gspec = pl.BlockSpec((HEADS_PER_GROUP, tm, HEAD_DIM), lambda i: (0, i, 0))
    kv_spec = pl.BlockSpec((MEM_TOKENS, MEM_W), lambda i: (0, 0))
    return pl.pallas_call(
        _mix_mem_prompt_kernel,
        grid=(m // tm,),
        in_specs=[pl.BlockSpec((tm, D_MODEL), lambda i: (i, 0))] + [gspec] * 6 + [
            pl.BlockSpec((tm, CONV_CH), lambda i: (i, 0)),
            pl.BlockSpec((D_MODEL, D_MODEL), lambda i: (0, 0), pipeline_mode=once),
            pl.BlockSpec((1, D_MODEL), lambda i: (0, 0)),
            pl.BlockSpec((D_MODEL, MEM_W), lambda i: (0, 0), pipeline_mode=once),
            pl.BlockSpec((1, HEAD_DIM), lambda i: (0, 0)),
            kv_spec, kv_spec,
            pl.BlockSpec((MEM_W, D_MODEL), lambda i: (0, 0), pipeline_mode=once),
        ],
        out_specs=pl.BlockSpec((tm, D_MODEL), lambda i: (i, 0)),
        out_shape=jax.ShapeDtypeStruct((m, D_MODEL), F32),
        scratch_shapes=[pltpu.VMEM((tm, D_MODEL), BF16), pltpu.VMEM((tm, MEM_W), BF16)],
        compiler_params=_cparams(("arbitrary",)),
        name="mix_mem_prompt",
    )(x, *os_, *ls_, conv, w_out_bf, nw, wq_bf, qn, k, v, wo_bf)


MEM_BB = 8
MEM_ROWS = MEM_BB * DEC_SEQ
MEM_KEYS = MEM_BB * MEM_TOKENS * MEM_HEADS


def _mem_attn_sample_kernel(x_ref, nw_ref, wq_ref, qn_ref, k_ref, v_ref, bias_ref, wo_ref, y_ref, o_scr):
    x = x_ref[...]
    q = _dot(_rms(x, nw_ref[...]).astype(BF16), wq_ref[...])
    qs = jnp.concatenate([_rms(q[:, h * HEAD_DIM:(h + 1) * HEAD_DIM], qn_ref[...]) for h in range(MEM_HEADS)],
                         axis=0).astype(BF16)
    s = _dot_nt(qs, k_ref[...].astype(BF16)) * SCALE + bias_ref[...]
    o = _softmax_pv(s, v_ref[...].astype(BF16))[0]
    for h in range(MEM_HEADS):
        o_scr[:, h * HEAD_DIM:(h + 1) * HEAD_DIM] = o[h * MEM_ROWS:(h + 1) * MEM_ROWS].astype(BF16)
    y_ref[...] = x + _dot(o_scr[...], wo_ref[...])


def _mem_sample_bias():
    r = np.arange(MEM_HEADS * MEM_ROWS)[:, None]
    c = np.arange(MEM_KEYS)[None, :]
    ok = (r // MEM_ROWS == c % MEM_HEADS) & ((r % MEM_ROWS) // DEC_SEQ == c // (MEM_TOKENS * MEM_HEADS))
    return np.where(ok, 0.0, NEG_INF).astype(np.float32)


def _mem_attn_sample(x, nw, wq_bf, qn, k_flat, v_flat, wo_bf):
    m = x.shape[0]
    bias = jnp.asarray(_mem_sample_bias())
    kv_spec = pl.BlockSpec((MEM_KEYS, HEAD_DIM), lambda i: (i, 0))
    return pl.pallas_call(
        _mem_attn_sample_kernel,
        grid=(m // MEM_ROWS,),
        in_specs=[
            pl.BlockSpec((MEM_ROWS, D_MODEL), lambda i: (i, 0)),
            pl.BlockSpec((1, D_MODEL), lambda i: (0, 0)),
            pl.BlockSpec((D_MODEL, MEM_W), lambda i: (0, 0)),
            pl.BlockSpec((1, HEAD_DIM), lambda i: (0, 0)),
            kv_spec, kv_spec,
            pl.BlockSpec(bias.shape, lambda i: (0, 0)),
            pl.BlockSpec((MEM_W, D_MODEL), lambda i: (0, 0)),
        ],
        out_specs=pl.BlockSpec((MEM_ROWS, D_MODEL), lambda i: (i, 0)),
        out_shape=jax.ShapeDtypeStruct((m, D_MODEL), F32),
        scratch_shapes=[pltpu.VMEM((MEM_ROWS, MEM_W), BF16)],
        compiler_params=_cparams(("arbitrary",)),
        name="mem_attn_sample",
    )(x, nw, wq_bf, qn, k_flat, v_flat, bias, wo_bf)


PEER_TQ = 128
NOT_TOP = 99.0
CAND_ROWS = 16 + 7 * 8 + 8
INVALID_FLAT = 999.0


def _cand_flat_index():
    flat = np.full((CAND_ROWS,), INVALID_FLAT, np.float32)
    flat[0:16] = np.arange(16)
    for a in range(1, 8):
        nb = TOPK // (a + 1)
        flat[16 + 8 * (a - 1):16 + 8 * (a - 1) + nb] = a * TOPK + np.arange(nb)
    flat[72:80] = np.arange(8, 16) * TOPK
    return np.broadcast_to(flat[:, None], (CAND_ROWS, PEER_TQ)).copy()


def _topk_rows(sa, sb, row, va_scr, vb_scr, ia_scr, ib_scr):
    for a in range(TOPK):
        ma = jnp.max(sa, axis=0, keepdims=True)
        mb = jnp.max(sb, axis=0, keepdims=True)
        ia = jnp.min(jnp.where(sa == ma, row, float(N_KEYS)), axis=0, keepdims=True)
        ib = jnp.min(jnp.where(sb == mb, row, float(N_KEYS)), axis=0, keepdims=True)
        sa = jnp.where(row == ia, NEG_INF, sa)
        sb = jnp.where(row == ib, NEG_INF, sb)
        va_scr[a:a + 1, :] = ma
        vb_scr[a:a + 1, :] = mb
        ia_scr[a:a + 1, :] = ia
        ib_scr[a:a + 1, :] = ib


def _peer_query_kernel(x_ref, nw_ref, wq_ref, k1_ref, k2_ref, flat_ref, *rest, with_tables):
    if with_tables:
        u_ref, v_ref, *rest = rest
        (h3t_ref, r2_ref, e2_ref, n1_ref, c1_ref, u_bf_ref, vt_bf_ref,
         q_scr, v1_scr, v2_scr, i1_scr, i2_scr) = rest
        u_bf_ref[...] = u_ref[...].astype(BF16)
        vt_bf_ref[...] = v_ref[...].T.astype(BF16)
    else:
        h3t_ref, r2_ref, e2_ref, n1_ref, c1_ref, q_scr, v1_scr, v2_scr, i1_scr, i2_scr = rest
    h3 = _rms(x_ref[...], nw_ref[...])
    h3t_ref[...] = h3.T.astype(BF16)
    q_scr[...] = _dot(h3.astype(BF16), wq_ref[...]).astype(BF16)
    row = lax.broadcasted_iota(jnp.int32, (N_KEYS, PEER_TQ), 0).astype(F32)
    flat = flat_ref[...]
    cand_ok = flat < INVALID_FLAT

    def head(h, carry):
        off = pl.multiple_of(h * 2 * HEAD_DIM, 2 * HEAD_DIM)
        s1 = _dot_nt(k1_ref[h], q_scr[:, pl.ds(off, HEAD_DIM)])
        s2 = _dot_nt(k2_ref[h], q_scr[:, pl.ds(off + HEAD_DIM, HEAD_DIM)])
        _topk_rows(s1, s2, row, v1_scr, v2_scr, i1_scr, i2_scr)
        pieces = [v1_scr[0:1, :] + v2_scr[...]]
        for a in range(1, 8):
            pieces.append(v1_scr[a:a + 1, :] + v2_scr[0:8, :])
        pieces.append(v1_scr[8:16, :] + v2_scr[0:1, :])
        cand0 = jnp.where(cand_ok, jnp.concatenate(pieces, axis=0), NEG_INF)
        cand = cand0
        for _ in range(TOPK):
            m = jnp.max(cand, axis=0, keepdims=True)
            f = jnp.min(jnp.where(cand == m, flat, 2 * INVALID_FLAT), axis=0, keepdims=True)
            cand = jnp.where(flat == f, NEG_INF, cand)
        sel = cand_ok & (cand == NEG_INF)
        z = jnp.sum(jnp.where(sel, jnp.exp(cand0 - cand0[0:1, :]), 0.0), axis=0, keepdims=True)
        self_ = sel.astype(F32)
        n1 = jnp.zeros((N_KEYS, PEER_TQ), F32)
        for a in range(TOPK):
            if a == 0:
                na = jnp.sum(self_[0:16, :], axis=0, keepdims=True)
            elif a < 8:
                na = jnp.sum(self_[16 + 8 * (a - 1):16 + 8 * a, :], axis=0, keepdims=True)
            else:
                na = self_[72 + a - 8:72 + a - 7, :]
            n1 = jnp.where(row == i1_scr[a:a + 1, :], na, n1)
        rank2 = jnp.full((N_KEYS, PEER_TQ), NOT_TOP, F32)
        for a in range(TOPK):
            rank2 = jnp.where(row == i2_scr[a:a + 1, :], float(a), rank2)
        r2_ref[h] = rank2.astype(BF16)
        e2_ref[h] = jnp.exp(s2 - v2_scr[0:1, :]).astype(BF16)
        n1_ref[h] = n1
        c1_ref[h] = jnp.exp(s1 - v1_scr[0:1, :]) / z
        return carry

    lax.fori_loop(0, PEER_HEADS, head, 0)


def _peer_query(x2, nw, wq_bf, k1_bf, k2_bf, tables=None):
    m = x2.shape[0]
    tq = PEER_TQ
    steps = m // tq
    flat = jnp.asarray(_cand_flat_index())
    key_spec = pl.BlockSpec((PEER_HEADS, N_KEYS, HEAD_DIM), lambda i: (0, 0, 0))
    stat_spec = pl.BlockSpec((PEER_HEADS, N_KEYS, tq), lambda i: (0, 0, i))
    stat_shape = lambda dt: jax.ShapeDtypeStruct((PEER_HEADS, N_KEYS, m), dt)
    table_in, table_out, table_shapes = [], [], []
    if tables is not None:
        rows = N_EXPERTS // steps
        table_in = [pl.BlockSpec((rows, D_MODEL), lambda i: (i, 0))] * 2
        table_out = [pl.BlockSpec((rows, D_MODEL), lambda i: (i, 0)), pl.BlockSpec((D_MODEL, rows), lambda i: (0, i))]
        table_shapes = [jax.ShapeDtypeStruct((N_EXPERTS, D_MODEL), BF16), jax.ShapeDtypeStruct((D_MODEL, N_EXPERTS), BF16)]
    return pl.pallas_call(
        functools.partial(_peer_query_kernel, with_tables=tables is not None),
        grid=(steps,),
        in_specs=[
            pl.BlockSpec((tq, D_MODEL), lambda i: (i, 0)),
            pl.BlockSpec((1, D_MODEL), lambda i: (0, 0)),
            pl.BlockSpec((D_MODEL, 2 * HEAD_DIM * PEER_HEADS), lambda i: (0, 0)),
            key_spec, key_spec,
            pl.BlockSpec((CAND_ROWS, tq), lambda i: (0, 0)),
        ] + table_in,
        out_specs=[pl.BlockSpec((D_MODEL, tq), lambda i: (0, i))] + [stat_spec] * 4 + table_out,
        out_shape=[jax.ShapeDtypeStruct((D_MODEL, m), BF16)] + [stat_shape(BF16)] * 2 + [stat_shape(F32)] * 2
        + table_shapes,
        scratch_shapes=[
            pltpu.VMEM((tq, 2 * HEAD_DIM * PEER_HEADS), BF16),
            pltpu.VMEM((TOPK, tq), F32),
            pltpu.VMEM((TOPK, tq), F32),
            pltpu.VMEM((TOPK, tq), F32),
            pltpu.VMEM((TOPK, tq), F32),
        ],
        compiler_params=_cparams(("arbitrary",)),
        name="peer_query",
    )(x2, nw, wq_bf, k1_bf, k2_bf, flat, *(tables or ()))


PEER_TM = 512
PEER_TE = 512
KEYS_PER_TILE = PEER_TE // N_KEYS
N_TILES = N_EXPERTS // PEER_TE
LANE = 128
BF16_ROWS = 16
INV_SQRT2 = 0.7071067811865476


def _gate_lanes(act_ref, wt_ref, r2_ref, e2_ref, n1_ref, c1_ref, key0, lc):
    ls = slice(lc * LANE, (lc + 1) * LANE)
    shape3 = (N_KEYS // BF16_ROWS, BF16_ROWS, LANE)
    gates = [jnp.zeros(shape3, BF16) for _ in range(KEYS_PER_TILE)]
    for h in range(PEER_HEADS):
        r2 = r2_ref[h, :, :, ls]
        e2 = e2_ref[h, :, :, ls]
        for c in range(KEYS_PER_TILE):
            n1 = jnp.broadcast_to(n1_ref[h, key0 + c:key0 + c + 1, ls], shape3[1:]).astype(BF16)
            c1 = jnp.broadcast_to(c1_ref[h, key0 + c:key0 + c + 1, ls], shape3[1:]).astype(BF16)
            gates[c] = gates[c] + (jnp.clip(n1 - r2, 0.0, 1.0) * e2) * c1
    for c in range(KEYS_PER_TILE):
        a = act_ref[c * N_KEYS:(c + 1) * N_KEYS, ls]
        gelu = 0.5 * a * (1.0 + lax.erf(a * INV_SQRT2))
        wt_ref[c * N_KEYS:(c + 1) * N_KEYS, ls] = (gates[c] * gelu.astype(BF16).reshape(shape3)).reshape(N_KEYS, LANE)


ACT_HALF = PEER_TM // 2
OUT_ROWS = D_MODEL // 4
assert D_MODEL // OUT_ROWS == PEER_TM // LANE


def _pipeline_stages(h3t_ref, acc_scr, stats, *, u_ref, act_out, act_in, key0, wt_out, wt_in, vt_ref):
    def act_piece(n):
        ls = slice(n * ACT_HALF, (n + 1) * ACT_HALF)
        act_out[:, ls] = _dot(u_ref[...], h3t_ref[:, ls])

    def out_piece(r):
        rs = slice(r * OUT_ROWS, (r + 1) * OUT_ROWS)
        acc_scr[rs, :] += _dot(vt_ref[rs, :], wt_in[...])

    def gate_piece(lc):
        _gate_lanes(act_in, wt_out, *stats, key0, lc)

    for r in range(D_MODEL // OUT_ROWS):
        out_piece(r)
        gate_piece(r)
    act_piece(0)
    act_piece(1)


def _peer_dense_kernel(h3t_ref, u0_ref, u_ref, vt_ref, vtl_ref, r2_ref, e2_ref, n1_ref, c1_ref,
                       x2_ref, y_ref, acc_scr, act0, act1, wt0, wt1):
    j = pl.program_id(1)
    stats = (r2_ref, e2_ref, n1_ref, c1_ref)

    @pl.when(j == 0)
    def _():
        acc_scr[...] = jnp.zeros_like(acc_scr)
        wt1[...] = jnp.zeros_like(wt1)
        act0[...] = _dot(u0_ref[...], h3t_ref[...])

    @pl.when(j % 2 == 0)
    def _():
        _pipeline_stages(h3t_ref, acc_scr, stats, u_ref=u_ref, act_out=act1, act_in=act0, key0=0,
                         wt_out=wt0, wt_in=wt1, vt_ref=vt_ref)

    @pl.when(j % 2 == 1)
    def _():
        _pipeline_stages(h3t_ref, acc_scr, stats, u_ref=u_ref, act_out=act0, act_in=act1, key0=KEYS_PER_TILE,
                         wt_out=wt1, wt_in=wt0, vt_ref=vt_ref)

    @pl.when(j == pl.num_programs(1) - 1)
    def _():
        y_ref[...] = x2_ref[...] + (acc_scr[...] + _dot(vtl_ref[...], wt1[...])).T


def _peer_dense(h3t, u_bf, vt_bf, r2, e2, n1, c1, x2):
    m = x2.shape[0]
    tm, te = PEER_TM, PEER_TE
    last = N_TILES - 1
    once = pl.Buffered(1)
    key2_shape = (PEER_HEADS, N_KEYS // BF16_ROWS, BF16_ROWS, m)
    r2, e2 = r2.reshape(key2_shape), e2.reshape(key2_shape)
    key2_spec = pl.BlockSpec(key2_shape[:3] + (tm,), lambda i, j: (0, 0, 0, i), pipeline_mode=once)
    key1_spec = pl.BlockSpec((PEER_HEADS, 2 * KEYS_PER_TILE, tm), lambda i, j: (0, j // 2, i))
    u_blk, vt_blk = (te, D_MODEL), (D_MODEL, te)
    return pl.pallas_call(
        _peer_dense_kernel,
        grid=(m // tm, N_TILES),
        in_specs=[
            pl.BlockSpec((D_MODEL, tm), lambda i, j: (0, i)),
            pl.BlockSpec(u_blk, lambda i, j: (0, 0), pipeline_mode=once),
            pl.BlockSpec(u_blk, lambda i, j: (jnp.minimum(j + 1, last), 0)),
            pl.BlockSpec(vt_blk, lambda i, j: (0, jnp.maximum(j - 1, 0))),
            pl.BlockSpec(vt_blk, lambda i, j: (0, last), pipeline_mode=once),
            key2_spec, key2_spec, key1_spec, key1_spec,
            pl.BlockSpec((tm, D_MODEL), lambda i, j: (i, 0), pipeline_mode=once),
        ],
        out_specs=pl.BlockSpec((tm, D_MODEL), lambda i, j: (i, 0)),
        out_shape=jax.ShapeDtypeStruct((m, D_MODEL), F32),
        scratch_shapes=[
            pltpu.VMEM((D_MODEL, tm), F32),
            pltpu.VMEM((te, tm), F32),
            pltpu.VMEM((te, tm), F32),
            pltpu.VMEM((te, tm), BF16),
            pltpu.VMEM((te, tm), BF16),
        ],
        compiler_params=_cparams(("arbitrary", "arbitrary")),
        name="peer_dense",
    )(h3t, u_bf, u_bf, vt_bf, vt_bf, r2, e2, n1, c1, x2)


def _peer(x2, nw, wq_bf, k1_bf, k2_bf, u_bf=None, vt_bf=None, tables=None):
    h3t, r2, e2, n1, c1, *made = _peer_query(x2, nw, wq_bf, k1_bf, k2_bf, tables)
    if tables is not None:
        u_bf, vt_bf = made
    return _peer_dense(h3t, u_bf, vt_bf, r2, e2, n1, c1, x2), u_bf, vt_bf


SAMPLE_BB = 1
SAMPLE_ROWS = SAMPLE_BB * DEC_SEQ
SAMPLE_Q = HEADS_PER_GROUP * SAMPLE_ROWS
NEW_PAD = 128
NEW_ROWS = DEC_SEQ * HEADS_PER_GROUP
SAMPLE_NKC = tuple(SAMPLE_BB * HEADS_PER_GROUP * n for n in (WINDOWS[0], WINDOWS[1], DEC_SEQ * N_BACK))
ROWS_PER_CHUNK = DILATIONS[2] * HEADS_PER_GROUP


def _sample_bias(g):
    nkc = SAMPLE_NKC[g]
    r = np.arange(SAMPLE_Q)[:, None]
    rh, rb, rt = r // SAMPLE_ROWS, (r % SAMPLE_ROWS) // DEC_SEQ, r % DEC_SEQ
    c = np.arange(nkc)[None, :]
    per_batch = nkc // SAMPLE_BB
    cb, ch = c // per_batch, c % HEADS_PER_GROUP
    if g == 0:
        ok_pos = (c % per_batch) // HEADS_PER_GROUP >= rt
    elif g == 1:
        ok_pos = ((c % per_batch) // HEADS_PER_GROUP) % DILATIONS[1] == rt
    else:
        ok_pos = (c % (DEC_SEQ * HEADS_PER_GROUP)) // HEADS_PER_GROUP == rt
    ok_c = (cb == rb) & (ch == rh) & ok_pos
    n = np.arange(NEW_PAD)[None, :]
    nh, nb, nt = n // SAMPLE_ROWS, (n % SAMPLE_ROWS) // DEC_SEQ, n % DEC_SEQ
    ok_n = (n < SAMPLE_Q) & (nh == rh) & (nb == rb) & ((nt <= rt) if g == 0 else (nt == rt))
    return np.where(np.concatenate([ok_c, ok_n], axis=1), 0.0, NEG_INF).astype(np.float32)


def _sample_attn_kernel(qkv_ref, new_ref, k0_ref, v0_ref, k1_ref, v1_ref, k2_ref, v2_ref, b0_ref, b1_ref, b2_ref,
                        o_ref, lse_ref, ok0_ref, ov0_ref, ok1_ref, ov1_ref, ok2_ref, ov2_ref):
    olds = (k0_ref, v0_ref, k1_ref, v1_ref, k2_ref, v2_ref)
    outs = (ok0_ref, ov0_ref, ok1_ref, ov1_ref, ok2_ref, ov2_ref)
    biases = (b0_ref, b1_ref, b2_ref)

    for c in range(2 * N_GROUPS):
        new = new_ref[0, c * NEW_ROWS:(c + 1) * NEW_ROWS, :]
        if c < 4:
            keep = olds[c].shape[1] - NEW_ROWS
            outs[c][0, 0:keep, :] = olds[c][0, NEW_ROWS:, :]
            outs[c][0, keep:, :] = new
        else:
            keep = ROWS_PER_CHUNK - NEW_ROWS
            last = olds[c].shape[0] - 1
            outs[c][:, 0:keep, :] = olds[c][:, NEW_ROWS:, :]
            outs[c][0:last, keep:, :] = olds[c][1:, 0:NEW_ROWS, :]
            outs[c][last, keep:, :] = new

    pad = jnp.zeros((NEW_PAD - SAMPLE_Q, HEAD_DIM), F32)
    for g in range(N_GROUPS):
        rows = slice(g * SAMPLE_Q, (g + 1) * SAMPLE_Q)
        q, kn, vn = (qkv_ref[0, w * N_GROUPS * SAMPLE_Q + g * SAMPLE_Q:w * N_GROUPS * SAMPLE_Q + (g + 1) * SAMPLE_Q, :]
                     for w in range(3))
        if g < 2:
            kc, vc = olds[2 * g][0], olds[2 * g + 1][0]
        else:
            kc = olds[4][:, 0:NEW_ROWS, :].reshape(SAMPLE_NKC[2], HEAD_DIM)
            vc = olds[5][:, 0:NEW_ROWS, :].reshape(SAMPLE_NKC[2], HEAD_DIM)
        k = jnp.concatenate([kc, kn, pad], axis=0).astype(BF16)
        v = jnp.concatenate([vc, vn, pad], axis=0).astype(BF16)
        s = _dot_nt(q.astype(BF16), k) * SCALE + biases[g][...]
        o, lse = _softmax_pv(s, v)
        o_ref[0, rows, :] = o
        lse_ref[0, rows, :] = jnp.broadcast_to(lse, (SAMPLE_Q, HEAD_DIM))


def _sample_attn(qkv_hm, caches):
    nbatch = qkv_hm.shape[1] // DEC_SEQ
    per_batch = qkv_hm.reshape(N_QKV_HEADS, nbatch, DEC_SEQ, HEAD_DIM).transpose(1, 0, 2, 3)
    new = per_batch[:, N_ATTN_HEADS:].reshape(nbatch, 2, N_GROUPS, HEADS_PER_GROUP, DEC_SEQ, HEAD_DIM)
    new = new.transpose(0, 2, 1, 4, 3, 5).reshape(nbatch, 2 * N_GROUPS * NEW_ROWS, HEAD_DIM)
    args = [per_batch.reshape(nbatch, N_QKV_HEADS * DEC_SEQ, HEAD_DIM), new]
    specs = [pl.BlockSpec((1,) + a.shape[1:], lambda i: (i, 0, 0)) for a in args]
    cache_specs, cache_shapes = [], []
    for g in range(N_GROUPS):
        for cache in caches[2 * g:2 * g + 2]:
            rows = cache.shape[1] * HEADS_PER_GROUP
            if g < 2:
                view = cache.reshape(nbatch, rows, HEAD_DIM)
                cache_specs.append(pl.BlockSpec((1, rows, HEAD_DIM), lambda i: (i, 0, 0)))
            else:
                view = cache.reshape(nbatch * rows // ROWS_PER_CHUNK, ROWS_PER_CHUNK, HEAD_DIM)
                cache_specs.append(pl.BlockSpec((rows // ROWS_PER_CHUNK, ROWS_PER_CHUNK, HEAD_DIM), lambda i: (i, 0, 0)))
            args.append(view)
            cache_shapes.append(jax.ShapeDtypeStruct(view.shape, F32))
    specs += cache_specs
    for g in range(N_GROUPS):
        bias = jnp.asarray(_sample_bias(g))
        args.append(bias)
        specs.append(pl.BlockSpec(bias.shape, lambda i: (0, 0)))
    out_spec = pl.BlockSpec((1, N_GROUPS * SAMPLE_Q, HEAD_DIM), lambda i: (i, 0, 0))
    out_shape = jax.ShapeDtypeStruct((nbatch, N_GROUPS * SAMPLE_Q, HEAD_DIM), F32)
    o, lse, *slid = pl.pallas_call(
        _sample_attn_kernel,
        grid=(nbatch,),
        in_specs=specs,
        out_specs=[out_spec, out_spec] + cache_specs,
        out_shape=[out_shape, out_shape] + cache_shapes,
        compiler_params=_cparams(("arbitrary",)),
        name="sample_attn",
    )(*args)
    head_major = lambda a: a.reshape(nbatch, N_ATTN_HEADS, DEC_SEQ, HEAD_DIM).transpose(1, 0, 2, 3).reshape(
        N_ATTN_HEADS, nbatch * DEC_SEQ, HEAD_DIM)
    return head_major(o), head_major(lse), [s_.reshape(c.shape) for s_, c in zip(slid, caches)]


def kernel(x_prompt, x_sample, mem_prompt, cache_k_w128, cache_v_w128, cache_k_w512, cache_v_w512, cache_k_w2048, cache_v_w2048, state_conv, cache_mem_k, cache_mem_v, norm_mix_w, w_in, q_norm_w, k_norm_w, conv_w, conv_b, conv_ln_w, conv_ln_b, w_out, norm_mem_w, mem_norm_w, wq_mem, wk_mem, wv_mem, qn_mem_w, kn_mem_w, wo_mem, norm_ffn_w, w_peer_q, peer_keys1, peer_keys2, peer_u, peer_v):
    s = x_prompt.shape[1]
    nbatch, t_new = x_sample.shape[:2]
    ms = nbatch * t_new

    row = lambda w: w.reshape(1, -1)
    w_in_bf = w_in.astype(BF16)
    w_out_bf = w_out.astype(BF16)
    wq_mem_bf, wk_mem_bf, wv_mem_bf, wo_mem_bf = (w.astype(BF16) for w in (wq_mem, wk_mem, wv_mem, wo_mem))
    wq_peer_bf = w_peer_q.astype(BF16)
    k1_bf = peer_keys1.astype(BF16)
    k2_bf = peer_keys2.astype(BF16)
    peer_w = (row(norm_ffn_w), wq_peer_bf, k1_bf, k2_bf)
    conv_vecs = (row(conv_b), row(conv_ln_w), row(conv_ln_b))

    xp = x_prompt.reshape(s, D_MODEL)
    cos_p, sin_p = _rope_tables(jnp.arange(s, dtype=jnp.int32))
    qkv_p, u_p = _project(xp, row(norm_mix_w), w_in_bf, row(q_norm_w), row(k_norm_w), cos_p, sin_p, tm=1024)
    attn = [_prompt_attn(qkv_p, g) for g in range(N_GROUPS)]
    conv_p = _prompt_conv(u_p, conv_w, *conv_vecs)
    mem_k, mem_v = _memory_kv(mem_prompt.reshape(MEM_TOKENS, D_MODEL), row(mem_norm_w), wk_mem_bf, wv_mem_bf,
                              row(kn_mem_w))
    x2_p = _mix_mem_prompt(xp, [a[0] for a in attn], [a[1] for a in attn], conv_p, w_out_bf,
                           row(norm_mem_w), wq_mem_bf, row(qn_mem_w), mem_k, mem_v, wo_mem_bf, tm=512)
    y_p, u_bf, vt_bf = _peer(x2_p, *peer_w, tables=(peer_u, peer_v))

    def group_heads(qkv_hm, which, g):
        h0 = which * N_ATTN_HEADS + g * HEADS_PER_GROUP
        return qkv_hm[h0:h0 + HEADS_PER_GROUP]

    p_win = []
    for g, win in enumerate(WINDOWS):
        keep = min(win, s)
        for which in (1, 2):
            heads = group_heads(qkv_p, which, g)[:, s - keep:]
            p_win.append(heads.transpose(1, 0, 2)[None])
    p_conv = u_p[s - (CONV_TAPS - 1):].reshape(1, CONV_TAPS - 1, CONV_CH)
    kv4 = lambda a: a.reshape(1, MEM_TOKENS, MEM_HEADS, HEAD_DIM)

    xs = x_sample.reshape(ms, D_MODEL)
    cos_s, sin_s = _rope_tables(PAST_LEN + jnp.arange(ms, dtype=jnp.int32) % t_new)
    qkv_s, u_s = _project(xs, row(norm_mix_w), w_in_bf, row(q_norm_w), row(k_norm_w), cos_s, sin_s, tm=ms)
    caches = (cache_k_w128, cache_v_w128, cache_k_w512, cache_v_w512, cache_k_w2048, cache_v_w2048)
    o_s, lse_s, s_win = _sample_attn(qkv_s, caches)

    n_state = CONV_TAPS - 1
    ucat = jnp.concatenate(
        [state_conv, u_s.reshape(nbatch, t_new, CONV_CH),
         jnp.zeros((nbatch, SAMPLE_CONV_ROWS - n_state - t_new, CONV_CH), F32)], axis=1)
    wsh = jnp.stack([jnp.pad(conv_w, ((t, SAMPLE_CONV_ROWS - CONV_TAPS - t), (0, 0))) for t in range(t_new)])
    conv_s = _sample_conv(ucat, wsh, *conv_vecs)
    conv_s = conv_s.transpose(1, 0, 2).reshape(ms, CONV_CH)
    s_conv = ucat[:, t_new:t_new + n_state]
    x1_s = _out_proj(xs, [o_s] * N_GROUPS, [lse_s] * N_GROUPS, (0, 1, 2), conv_s, w_out_bf, tm=ms)
    x2_s = _mem_attn_sample(x1_s, row(norm_mem_w), wq_mem_bf, row(qn_mem_w),
                            cache_mem_k.reshape(-1, HEAD_DIM), cache_mem_v.reshape(-1, HEAD_DIM), wo_mem_bf)
    y_s = _peer(x2_s, *peer_w, u_bf, vt_bf)[0]

    return (y_p.reshape(x_prompt.shape), y_s.reshape(x_sample.shape), *p_win, p_conv, kv4(mem_k), kv4(mem_v),
            *s_win, s_conv)
```

```python
import functools

import numpy as np
import jax
import jax.numpy as jnp
from jax import lax
from jax.experimental import pallas as pl
from jax.experimental.pallas import tpu as pltpu

F32 = jnp.float32
BF16 = jnp.bfloat16

D_MODEL = 2048
HEAD_DIM = 128
HEADS_PER_GROUP = 4
GROUP_W = HEADS_PER_GROUP * HEAD_DIM
DILATIONS = (1, 4, 16)
WINDOWS = (128, 512, 2048)
N_BACK = 128
N_GROUPS = 3
ATTN_W = N_GROUPS * GROUP_W
CONV_CH = 512
CONV_TAPS = 31
QKV_W = 3 * ATTN_W
N_ATTN_HEADS = N_GROUPS * HEADS_PER_GROUP
N_QKV_HEADS = 3 * N_ATTN_HEADS
MEM_TOKENS = 256
MEM_W = 512
PEER_HEADS = 8
N_KEYS = 128
N_EXPERTS = N_KEYS * N_KEYS
TOPK = 16
PAST_LEN = 2048
DEC_SEQ = 4
ROPE_THETA = 10000.0
EPS = 1e-6
SCALE = HEAD_DIM ** -0.5
NEG_INF = float("-inf")

VMEM_LIMIT = 56 * 1024 * 1024


def _cparams(sem):
    return pltpu.CompilerParams(dimension_semantics=sem, vmem_limit_bytes=VMEM_LIMIT)


def _rms(x, w):
    return x * lax.rsqrt(jnp.mean(x * x, axis=-1, keepdims=True) + EPS) * w


def _dot(a, b):
    return jnp.dot(a, b, preferred_element_type=F32)


def _dot_nt(a, b):
    return lax.dot_general(a, b, (((1,), (1,)), ((), ())), preferred_element_type=F32)


def _softmax_pv(s, v):
    m = jnp.max(s, axis=-1, keepdims=True)
    p = jnp.exp(s - m)
    den = jnp.sum(p, axis=-1, keepdims=True)
    return _dot(p.astype(BF16), v) / den, m + jnp.log(den)


def _proj_kernel(x_ref, nw_ref, w_ref, wa_ref, wg_ref, qn_ref, kn_ref, cos_ref, sin_ref,
                 qkv_ref, u_ref, h_scr):
    j = pl.program_id(1)

    @pl.when(j == 0)
    def _():
        h_scr[...] = _rms(x_ref[...], nw_ref[...]).astype(BF16)
        a = _dot(h_scr[...], wa_ref[...])
        g = _dot(h_scr[...], wg_ref[...])
        u_ref[...] = a * jax.nn.sigmoid(g)

    z = _dot(h_scr[...], w_ref[...])

    @pl.when(j < 2 * N_GROUPS)
    def _():
        w = jnp.where(j < N_GROUPS, qn_ref[...], kn_ref[...])
        cos = cos_ref[...]
        sin = sin_ref[...]
        for h in range(HEADS_PER_GROUP):
            y = _rms(z[:, h * HEAD_DIM:(h + 1) * HEAD_DIM], w)
            qkv_ref[h] = y * cos + pltpu.roll(y, HEAD_DIM // 2, 1) * sin

    @pl.when(j >= 2 * N_GROUPS)
    def _():
        for h in range(HEADS_PER_GROUP):
            qkv_ref[h] = z[:, h * HEAD_DIM:(h + 1) * HEAD_DIM]


def _project(x, nw, w_in_bf, qn, kn, cos, sin, tm):
    m = x.shape[0]
    n_tiles = QKV_W // GROUP_W
    return pl.pallas_call(
        _proj_kernel,
        grid=(m // tm, n_tiles),
        in_specs=[
            pl.BlockSpec((tm, D_MODEL), lambda i, j: (i, 0)),
            pl.BlockSpec((1, D_MODEL), lambda i, j: (0, 0)),
            pl.BlockSpec((D_MODEL, GROUP_W), lambda i, j: (0, j)),
            pl.BlockSpec((D_MODEL, GROUP_W), lambda i, j: (0, n_tiles)),
            pl.BlockSpec((D_MODEL, GROUP_W), lambda i, j: (0, n_tiles + 1)),
            pl.BlockSpec((1, HEAD_DIM), lambda i, j: (0, 0)),
            pl.BlockSpec((1, HEAD_DIM), lambda i, j: (0, 0)),
            pl.BlockSpec((tm, HEAD_DIM), lambda i, j: (i, 0)),
            pl.BlockSpec((tm, HEAD_DIM), lambda i, j: (i, 0)),
        ],
        out_specs=[
            pl.BlockSpec((HEADS_PER_GROUP, tm, HEAD_DIM), lambda i, j: (j, i, 0)),
            pl.BlockSpec((tm, GROUP_W), lambda i, j: (i, 0)),
        ],
        out_shape=[
            jax.ShapeDtypeStruct((N_QKV_HEADS, m, HEAD_DIM), F32),
            jax.ShapeDtypeStruct((m, CONV_CH), F32),
        ],
        scratch_shapes=[pltpu.VMEM((tm, D_MODEL), BF16)],
        compiler_params=_cparams(("arbitrary", "arbitrary")),
        name="proj",
    )(x, nw, w_in_bf, w_in_bf, w_in_bf, qn, kn, cos, sin)


def _rope_tables(pos):
    half = HEAD_DIM // 2
    inv = ROPE_THETA ** (-jnp.arange(half, dtype=F32) / half)
    ang = pos.astype(F32)[:, None] * inv[None, :]
    cos = jnp.cos(ang)
    sin = jnp.sin(ang)
    return jnp.concatenate([cos, cos], axis=-1), jnp.concatenate([-sin, sin], axis=-1)


QUERY_BLOCK = 128


def _prompt_attn_kernel(q_ref, k_ref, v_ref, o_ref, lse_ref, kprev, vprev, *, dil):
    i = pl.program_id(0)
    qb = QUERY_BLOCK

    @pl.when(i == 0)
    def _():
        kprev[...] = jnp.zeros_like(kprev)
        vprev[...] = jnp.zeros_like(vprev)

    qi = lax.broadcasted_iota(jnp.int32, (qb, 2 * qb), 0)
    kj = lax.broadcasted_iota(jnp.int32, (qb, 2 * qb), 1)
    valid = (kj >= qi + (qb - N_BACK)) & (kj <= qi + qb) & ((kj >= qb) | (i > 0))
    for h in range(HEADS_PER_GROUP):
        for r in range(dil):
            rows = pl.ds(r, qb, stride=dil) if dil > 1 else pl.ds(0, qb)
            q = q_ref[h, rows, :].astype(BF16)
            k = jnp.concatenate([kprev[h, rows, :], k_ref[h, rows, :]], axis=0).astype(BF16)
            v = jnp.concatenate([vprev[h, rows, :], v_ref[h, rows, :]], axis=0).astype(BF16)
            o, lse = _softmax_pv(jnp.where(valid, _dot_nt(q, k) * SCALE, NEG_INF), v)
            o_ref[h, rows, :] = o
            lse_ref[h, rows, :] = jnp.broadcast_to(lse, (qb, HEAD_DIM))
    kprev[...] = k_ref[...]
    vprev[...] = v_ref[...]


def _prompt_attn(qkv_hm, g):
    s = qkv_hm.shape[1]
    dil = DILATIONS[g]
    sb = QUERY_BLOCK * dil
    blk = (HEADS_PER_GROUP, sb, HEAD_DIM)
    out_spec = pl.BlockSpec(blk, lambda i: (0, i, 0))
    out_shape = jax.ShapeDtypeStruct((HEADS_PER_GROUP, s, HEAD_DIM), F32)
    return pl.pallas_call(
        functools.partial(_prompt_attn_kernel, dil=dil),
        grid=(s // sb,),
        in_specs=[
            pl.BlockSpec(blk, lambda i: (g, i, 0)),
            pl.BlockSpec(blk, lambda i: (N_GROUPS + g, i, 0)),
            pl.BlockSpec(blk, lambda i: (2 * N_GROUPS + g, i, 0)),
        ],
        out_specs=[out_spec, out_spec],
        out_shape=[out_shape, out_shape],
        scratch_shapes=[pltpu.VMEM(blk, F32), pltpu.VMEM(blk, F32)],
        compiler_params=_cparams(("arbitrary",)),
        name=f"prompt_attn_g{g}",
    )(qkv_hm, qkv_hm, qkv_hm)


def _ln_swish(y, lnw, lnb):
    mu = jnp.mean(y, axis=-1, keepdims=True)
    yc = y - mu
    yn = yc * lax.rsqrt(jnp.mean(yc * yc, axis=-1, keepdims=True) + EPS) * lnw + lnb
    return yn * jax.nn.sigmoid(yn)


HALO = 32


SUBLANES = 8


def _prompt_conv_kernel(u_ref, halo_ref, cw_ref, cb_ref, lnw_ref, lnb_ref, o_ref, scr, *shifted):
    i = pl.program_id(0)
    tm = u_ref.shape[0]
    scr[0:HALO, :] = jnp.where(i > 0, halo_ref[...], 0.0)
    scr[HALO:, :] = u_ref[...]
    first = HALO - (CONV_TAPS - 1)
    acc = jnp.zeros((tm, CONV_CH), F32)
    for b in range(SUBLANES):
        src = scr
        if b:
            src = shifted[b - 1]
            src[...] = scr[pl.ds(b, src.shape[0]), :]
        for a in range((CONV_TAPS - 1 + first) // SUBLANES + 1):
            j = SUBLANES * a + b - first
            if 0 <= j < CONV_TAPS:
                acc = acc + cw_ref[j:j + 1, :] * src[pl.ds(SUBLANES * a, tm), :]
    o_ref[...] = _ln_swish(acc + cb_ref[...], lnw_ref[...], lnb_ref[...])


def _prompt_conv(u, cw, cb, lnw, lnb, tm=512):
    m = u.shape[0]
    vec = pl.BlockSpec((1, CONV_CH), lambda i: (0, 0))
    return pl.pallas_call(
        _prompt_conv_kernel,
        grid=(m // tm,),
        in_specs=[
            pl.BlockSpec((tm, CONV_CH), lambda i: (i, 0)),
            pl.BlockSpec((HALO, CONV_CH), lambda i: (jnp.maximum(i * (tm // HALO) - 1, 0), 0)),
            pl.BlockSpec((CONV_TAPS, CONV_CH), lambda i: (0, 0)),
            vec, vec, vec,
        ],
        out_specs=pl.BlockSpec((tm, CONV_CH), lambda i: (i, 0)),
        out_shape=jax.ShapeDtypeStruct((m, CONV_CH), F32),
        scratch_shapes=[pltpu.VMEM((tm + HALO, CONV_CH), F32)]
        + [pltpu.VMEM((tm + HALO - SUBLANES, CONV_CH), F32)] * (SUBLANES - 1),
        compiler_params=_cparams(("arbitrary",)),
        name="prompt_conv",
    )(u, u, cw, cb, lnw, lnb)


SAMPLE_CONV_ROWS = 40


def _sample_conv_kernel(ucat_ref, wsh_ref, cb_ref, lnw_ref, lnb_ref, o_ref):
    ucat = ucat_ref[...]
    for t in range(DEC_SEQ):
        y = jnp.sum(ucat * wsh_ref[t][None], axis=1)
        o_ref[t] = _ln_swish(y + cb_ref[...], lnw_ref[...], lnb_ref[...])


def _sample_conv(ucat, wsh, cb, lnw, lnb):
    nb = ucat.shape[0]
    return pl.pallas_call(
        _sample_conv_kernel,
        out_shape=jax.ShapeDtypeStruct((DEC_SEQ, nb, CONV_CH), F32),
        compiler_params=pltpu.CompilerParams(vmem_limit_bytes=VMEM_LIMIT),
        name="sample_conv",
    )(ucat, wsh, cb, lnw, lnb)


def _combine_and_project(x_ref, o_refs, l_refs, cv_ref, w_ref, a_scr):
    for h in range(HEADS_PER_GROUP):
        l0, l1, l2 = (l_ref[h] for l_ref in l_refs)
        lm = jnp.maximum(jnp.maximum(l0, l1), l2)
        es = (jnp.exp(l0 - lm), jnp.exp(l1 - lm), jnp.exp(l2 - lm))
        den = es[0] + es[1] + es[2]
        for g in range(N_GROUPS):
            c0 = g * GROUP_W + h * HEAD_DIM
            a_scr[:, c0:c0 + HEAD_DIM] = ((es[g] / den) * o_refs[g][h]).astype(BF16)
    a_scr[:, ATTN_W:] = cv_ref[...].astype(BF16)
    return x_ref[...] + _dot(a_scr[...], w_ref[...])


def _out_proj_kernel(x_ref, o0_ref, o1_ref, o2_ref, l0_ref, l1_ref, l2_ref, cv_ref, w_ref, y_ref, a_scr):
    y_ref[...] = _combine_and_project(x_ref, (o0_ref, o1_ref, o2_ref), (l0_ref, l1_ref, l2_ref), cv_ref, w_ref, a_scr)


def _out_proj(x, os_, ls_, gidx, conv, w_out_bf, tm):
    m = x.shape[0]
    gspecs = [pl.BlockSpec((HEADS_PER_GROUP, tm, HEAD_DIM), lambda i, gi=gi: (gi, i, 0)) for gi in gidx]
    return pl.pallas_call(
        _out_proj_kernel,
        grid=(m // tm,),
        in_specs=[pl.BlockSpec((tm, D_MODEL), lambda i: (i, 0))] + gspecs + gspecs
        + [pl.BlockSpec((tm, CONV_CH), lambda i: (i, 0)), pl.BlockSpec((D_MODEL, D_MODEL), lambda i: (0, 0))],
        out_specs=pl.BlockSpec((tm, D_MODEL), lambda i: (i, 0)),
        out_shape=jax.ShapeDtypeStruct((m, D_MODEL), F32),
        scratch_shapes=[pltpu.VMEM((tm, D_MODEL), BF16)],
        compiler_params=_cparams(("arbitrary",)),
        name="out_proj",
    )(x, *os_, *ls_, conv, w_out_bf)


def _memory_kv_kernel(mem_ref, nw_ref, wk_ref, wv_ref, kn_ref, k_ref, v_ref):
    h = _rms(mem_ref[...], nw_ref[...]).astype(BF16)
    k = _dot(h, wk_ref[...])
    v_ref[...] = _dot(h, wv_ref[...])
    for hd in range(MEM_W // HEAD_DIM):
        hs = slice(hd * HEAD_DIM, (hd + 1) * HEAD_DIM)
        k_ref[:, hs] = _rms(k[:, hs], kn_ref[...])


def _memory_kv(mem, nw, wk_bf, wv_bf, kn):
    return pl.pallas_call(
        _memory_kv_kernel,
        out_shape=[jax.ShapeDtypeStruct((MEM_TOKENS, MEM_W), F32)] * 2,
        compiler_params=pltpu.CompilerParams(vmem_limit_bytes=VMEM_LIMIT),
        name="memory_kv",
    )(mem, nw, wk_bf, wv_bf, kn)


MEM_HEADS = MEM_W // HEAD_DIM


def _mix_mem_prompt_kernel(x_ref, o0_ref, o1_ref, o2_ref, l0_ref, l1_ref, l2_ref, cv_ref, w_ref,
                           nw_ref, wq_ref, qn_ref, k_ref, v_ref, wo_ref, y_ref, a_scr, o_scr):
    x1 = _combine_and_project(x_ref, (o0_ref, o1_ref, o2_ref), (l0_ref, l1_ref, l2_ref), cv_ref, w_ref, a_scr)
    q = _dot(_rms(x1, nw_ref[...]).astype(BF16), wq_ref[...])
    for h in range(MEM_HEADS):
        hs = slice(h * HEAD_DIM, (h + 1) * HEAD_DIM)
        qh = _rms(q[:, hs], qn_ref[...]).astype(BF16)
        s = _dot_nt(qh, k_ref[:, hs].astype(BF16)) * SCALE
        o_scr[:, hs] = _softmax_pv(s, v_ref[:, hs].astype(BF16))[0].astype(BF16)
    y_ref[...] = x1 + _dot(o_scr[...], wo_ref[...])


def _mix_mem_prompt(x, os_, ls_, conv, w_out_bf, nw, wq_bf, qn, k, v, wo_bf, tm):
    m = x.shape[0]
    once = pl.Buffered(1)
    gspec = pl.BlockSpec((HEADS_PER_GROUP, tm, HEAD_DIM), lambda i: (0, i, 0))
    kv_spec = pl.BlockSpec((MEM_TOKENS, MEM_W), lambda i: (0, 0))
    return pl.pallas_call(
        _mix_mem_prompt_kernel,
        grid=(m // tm,),
        in_specs=[pl.BlockSpec((tm, D_MODEL), lambda i: (i, 0))] + [gspec] * 6 + [
            pl.BlockSpec((tm, CONV_CH), lambda i: (i, 0)),
            pl.BlockSpec((D_MODEL, D_MODEL), lambda i: (0, 0), pipeline_mode=once),
            pl.BlockSpec((1, D_MODEL), lambda i: (0, 0)),
            pl.BlockSpec((D_MODEL, MEM_W), lambda i: (0, 0), pipeline_mode=once),
            pl.BlockSpec((1, HEAD_DIM), lambda i: (0, 0)),
            kv_spec, kv_spec,
            pl.BlockSpec((MEM_W, D_MODEL), lambda i: (0, 0), pipeline_mode=once),
        ],
        out_specs=pl.BlockSpec((tm, D_MODEL), lambda i: (i, 0)),
        out_shape=jax.ShapeDtypeStruct((m, D_MODEL), F32),
        scratch_shapes=[pltpu.VMEM((tm, D_MODEL), BF16), pltpu.VMEM((tm, MEM_W), BF16)],
        compiler_params=_cparams(("arbitrary",)),
        name="mix_mem_prompt",
    )(x, *os_, *ls_, conv, w_out_bf, nw, wq_bf, qn, k, v, wo_bf)


MEM_BB = 8
MEM_ROWS = MEM_BB * DEC_SEQ
MEM_KEYS = MEM_BB * MEM_TOKENS * MEM_HEADS


def _mem_attn_sample_kernel(x_ref, nw_ref, wq_ref, qn_ref, k_ref, v_ref, bias_ref, wo_ref, y_ref, o_scr):
    x = x_ref[...]
    q = _dot(_rms(x, nw_ref[...]).astype(BF16), wq_ref[...])
    qs = jnp.concatenate([_rms(q[:, h * HEAD_DIM:(h + 1) * HEAD_DIM], qn_ref[...]) for h in range(MEM_HEADS)],
                         axis=0).astype(BF16)
    s = _dot_nt(qs, k_ref[...].astype(BF16)) * SCALE + bias_ref[...]
    o = _softmax_pv(s, v_ref[...].astype(BF16))[0]
    for h in range(MEM_HEADS):
        o_scr[:, h * HEAD_DIM:(h + 1) * HEAD_DIM] = o[h * MEM_ROWS:(h + 1) * MEM_ROWS].astype(BF16)
    y_ref[...] = x + _dot(o_scr[...], wo_ref[...])


def _mem_sample_bias():
    r = np.arange(MEM_HEADS * MEM_ROWS)[:, None]
    c = np.arange(MEM_KEYS)[None, :]
    ok = (r // MEM_ROWS == c % MEM_HEADS) & ((r % MEM_ROWS) // DEC_SEQ == c // (MEM_TOKENS * MEM_HEADS))
    return np.where(ok, 0.0, NEG_INF).astype(np.float32)


def _mem_attn_sample(x, nw, wq_bf, qn, k_flat, v_flat, wo_bf):
    m = x.shape[0]
    bias = jnp.asarray(_mem_sample_bias())
    kv_spec = pl.BlockSpec((MEM_KEYS, HEAD_DIM), lambda i: (i, 0))
    return pl.pallas_call(
        _mem_attn_sample_kernel,
        grid=(m // MEM_ROWS,),
        in_specs=[
            pl.BlockSpec((MEM_ROWS, D_MODEL), lambda i: (i, 0)),
            pl.BlockSpec((1, D_MODEL), lambda i: (0, 0)),
            pl.BlockSpec((D_MODEL, MEM_W), lambda i: (0, 0)),
            pl.BlockSpec((1, HEAD_DIM), lambda i: (0, 0)),
            kv_spec, kv_spec,
            pl.BlockSpec(bias.shape, lambda i: (0, 0)),
            pl.BlockSpec((MEM_W, D_MODEL), lambda i: (0, 0)),
        ],
        out_specs=pl.BlockSpec((MEM_ROWS, D_MODEL), lambda i: (i, 0)),
        out_shape=jax.ShapeDtypeStruct((m, D_MODEL), F32),
        scratch_shapes=[pltpu.VMEM((MEM_ROWS, MEM_W), BF16)],
        compiler_params=_cparams(("arbitrary",)),
        name="mem_attn_sample",
    )(x, nw, wq_bf, qn, k_flat, v_flat, bias, wo_bf)


PEER_TQ = 128
NOT_TOP = 99.0
CAND_ROWS = 16 + 7 * 8 + 8
INVALID_FLAT = 999.0


def _cand_flat_index():
    flat = np.full((CAND_ROWS,), INVALID_FLAT, np.float32)
    flat[0:16] = np.arange(16)
    for a in range(1, 8):
        nb = TOPK // (a + 1)
        flat[16 + 8 * (a - 1):16 + 8 * (a - 1) + nb] = a * TOPK + np.arange(nb)
    flat[72:80] = np.arange(8, 16) * TOPK
    return np.broadcast_to(flat[:, None], (CAND_ROWS, PEER_TQ)).copy()


def _topk_rows(sa, sb, row, va_scr, vb_scr, ia_scr, ib_scr):
    for a in range(TOPK):
        ma = jnp.max(sa, axis=0, keepdims=True)
        mb = jnp.max(sb, axis=0, keepdims=True)
        ia = jnp.min(jnp.where(sa == ma, row, float(N_KEYS)), axis=0, keepdims=True)
        ib = jnp.min(jnp.where(sb == mb, row, float(N_KEYS)), axis=0, keepdims=True)
        sa = jnp.where(row == ia, NEG_INF, sa)
        sb = jnp.where(row == ib, NEG_INF, sb)
        va_scr[a:a + 1, :] = ma
        vb_scr[a:a + 1, :] = mb
        ia_scr[a:a + 1, :] = ia
        ib_scr[a:a + 1, :] = ib


def _peer_query_kernel(x_ref, nw_ref, wq_ref, k1_ref, k2_ref, flat_ref, *rest, with_tables):
    if with_tables:
        u_ref, v_ref, *rest = rest
        (h3t_ref, r2_ref, e2_ref, n1_ref, c1_ref, u_bf_ref, vt_bf_ref,
         q_scr, v1_scr, v2_scr, i1_scr, i2_scr) = rest
        u_bf_ref[...] = u_ref[...].astype(BF16)
        vt_bf_ref[...] = v_ref[...].T.astype(BF16)
    else:
        h3t_ref, r2_ref, e2_ref, n1_ref, c1_ref, q_scr, v1_scr, v2_scr, i1_scr, i2_scr = rest
    h3 = _rms(x_ref[...], nw_ref[...])
    h3t_ref[...] = h3.T.astype(BF16)
    q_scr[...] = _dot(h3.astype(BF16), wq_ref[...]).astype(BF16)
    row = lax.broadcasted_iota(jnp.int32, (N_KEYS, PEER_TQ), 0).astype(F32)
    flat = flat_ref[...]
    cand_ok = flat < INVALID_FLAT

    def head(h, carry):
        off = pl.multiple_of(h * 2 * HEAD_DIM, 2 * HEAD_DIM)
        s1 = _dot_nt(k1_ref[h], q_scr[:, pl.ds(off, HEAD_DIM)])
        s2 = _dot_nt(k2_ref[h], q_scr[:, pl.ds(off + HEAD_DIM, HEAD_DIM)])
        _topk_rows(s1, s2, row, v1_scr, v2_scr, i1_scr, i2_scr)
        pieces = [v1_scr[0:1, :] + v2_scr[...]]
        for a in range(1, 8):
            pieces.append(v1_scr[a:a + 1, :] + v2_scr[0:8, :])
        pieces.append(v1_scr[8:16, :] + v2_scr[0:1, :])
        cand0 = jnp.where(cand_ok, jnp.concatenate(pieces, axis=0), NEG_INF)
        cand = cand0
        for _ in range(TOPK):
            m = jnp.max(cand, axis=0, keepdims=True)
            f = jnp.min(jnp.where(cand == m, flat, 2 * INVALID_FLAT), axis=0, keepdims=True)
            cand = jnp.where(flat == f, NEG_INF, cand)
        sel = cand_ok & (cand == NEG_INF)
        z = jnp.sum(jnp.where(sel, jnp.exp(cand0 - cand0[0:1, :]), 0.0), axis=0, keepdims=True)
        self_ = sel.astype(F32)
        n1 = jnp.zeros((N_KEYS, PEER_TQ), F32)
        for a in range(TOPK):
            if a == 0:
                na = jnp.sum(self_[0:16, :], axis=0, keepdims=True)
            elif a < 8:
                na = jnp.sum(self_[16 + 8 * (a - 1):16 + 8 * a, :], axis=0, keepdims=True)
            else:
                na = self_[72 + a - 8:72 + a - 7, :]
            n1 = jnp.where(row == i1_scr[a:a + 1, :], na, n1)
        rank2 = jnp.full((N_KEYS, PEER_TQ), NOT_TOP, F32)
        for a in range(TOPK):
            rank2 = jnp.where(row == i2_scr[a:a + 1, :], float(a), rank2)
        r2_ref[h] = rank2.astype(BF16)
        e2_ref[h] = jnp.exp(s2 - v2_scr[0:1, :]).astype(BF16)
        n1_ref[h] = n1
        c1_ref[h] = jnp.exp(s1 - v1_scr[0:1, :]) / z
        return carry

    lax.fori_loop(0, PEER_HEADS, head, 0)


def _peer_query(x2, nw, wq_bf, k1_bf, k2_bf, tables=None):
    m = x2.shape[0]
    tq = PEER_TQ
    steps = m // tq
    flat = jnp.asarray(_cand_flat_index())
    key_spec = pl.BlockSpec((PEER_HEADS, N_KEYS, HEAD_DIM), lambda i: (0, 0, 0))
    stat_spec = pl.BlockSpec((PEER_HEADS, N_KEYS, tq), lambda i: (0, 0, i))
    stat_shape = lambda dt: jax.ShapeDtypeStruct((PEER_HEADS, N_KEYS, m), dt)
    table_in, table_out, table_shapes = [], [], []
    if tables is not None:
        rows = N_EXPERTS // steps
        table_in = [pl.BlockSpec((rows, D_MODEL), lambda i: (i, 0))] * 2
        table_out = [pl.BlockSpec((rows, D_MODEL), lambda i: (i, 0)), pl.BlockSpec((D_MODEL, rows), lambda i: (0, i))]
        table_shapes = [jax.ShapeDtypeStruct((N_EXPERTS, D_MODEL), BF16), jax.ShapeDtypeStruct((D_MODEL, N_EXPERTS), BF16)]
    return pl.pallas_call(
        functools.partial(_peer_query_kernel, with_tables=tables is not None),
        grid=(steps,),
        in_specs=[
            pl.BlockSpec((tq, D_MODEL), lambda i: (i, 0)),
            pl.BlockSpec((1, D_MODEL), lambda i: (0, 0)),
            pl.BlockSpec((D_MODEL, 2 * HEAD_DIM * PEER_HEADS), lambda i: (0, 0)),
            key_spec, key_spec,
            pl.BlockSpec((CAND_ROWS, tq), lambda i: (0, 0)),
        ] + table_in,
        out_specs=[pl.BlockSpec((D_MODEL, tq), lambda i: (0, i))] + [stat_spec] * 4 + table_out,
        out_shape=[jax.ShapeDtypeStruct((D_MODEL, m), BF16)] + [stat_shape(BF16)] * 2 + [stat_shape(F32)] * 2
        + table_shapes,
        scratch_shapes=[
            pltpu.VMEM((tq, 2 * HEAD_DIM * PEER_HEADS), BF16),
            pltpu.VMEM((TOPK, tq), F32),
            pltpu.VMEM((TOPK, tq), F32),
            pltpu.VMEM((TOPK, tq), F32),
            pltpu.VMEM((TOPK, tq), F32),
        ],
        compiler_params=_cparams(("arbitrary",)),
        name="peer_query",
    )(x2, nw, wq_bf, k1_bf, k2_bf, flat, *(tables or ()))


PEER_TM = 512
PEER_TE = 512
KEYS_PER_TILE = PEER_TE // N_KEYS
N_TILES = N_EXPERTS // PEER_TE
LANE = 128
BF16_ROWS = 16
INV_SQRT2 = 0.7071067811865476


def _gate_lanes(act_ref, wt_ref, r2_ref, e2_ref, n1_ref, c1_ref, key0, lc):
    ls = slice(lc * LANE, (lc + 1) * LANE)
    shape3 = (N_KEYS // BF16_ROWS, BF16_ROWS, LANE)
    gates = [jnp.zeros(shape3, BF16) for _ in range(KEYS_PER_TILE)]
    for h in range(PEER_HEADS):
        r2 = r2_ref[h, :, :, ls]
        e2 = e2_ref[h, :, :, ls]
        for c in range(KEYS_PER_TILE):
            n1 = jnp.broadcast_to(n1_ref[h, key0 + c:key0 + c + 1, ls], shape3[1:]).astype(BF16)
            c1 = jnp.broadcast_to(c1_ref[h, key0 + c:key0 + c + 1, ls], shape3[1:]).astype(BF16)
            gates[c] = gates[c] + (jnp.clip(n1 - r2, 0.0, 1.0) * e2) * c1
    for c in range(KEYS_PER_TILE):
        a = act_ref[c * N_KEYS:(c + 1) * N_KEYS, ls]
        gelu = 0.5 * a * (1.0 + lax.erf(a * INV_SQRT2))
        wt_ref[c * N_KEYS:(c + 1) * N_KEYS, ls] = (gates[c] * gelu.astype(BF16).reshape(shape3)).reshape(N_KEYS, LANE)


ACT_HALF = PEER_TM // 2
OUT_ROWS = D_MODEL // 4
assert D_MODEL // OUT_ROWS == PEER_TM // LANE


def _pipeline_stages(h3t_ref, acc_scr, stats, *, u_ref, act_out, act_in, key0, wt_out, wt_in, vt_ref):
    def act_piece(n):
        ls = slice(n * ACT_HALF, (n + 1) * ACT_HALF)
        act_out[:, ls] = _dot(u_ref[...], h3t_ref[:, ls])

    def out_piece(r):
        rs = slice(r * OUT_ROWS, (r + 1) * OUT_ROWS)
        acc_scr[rs, :] += _dot(vt_ref[rs, :], wt_in[...])

    def gate_piece(lc):
        _gate_lanes(act_in, wt_out, *stats, key0, lc)

    for r in range(D_MODEL // OUT_ROWS):
        out_piece(r)
        gate_piece(r)
    act_piece(0)
    act_piece(1)


def _peer_dense_kernel(h3t_ref, u0_ref, u_ref, vt_ref, vtl_ref, r2_ref, e2_ref, n1_ref, c1_ref,
                       x2_ref, y_ref, acc_scr, act0, act1, wt0, wt1):
    j = pl.program_id(1)
    stats = (r2_ref, e2_ref, n1_ref, c1_ref)

    @pl.when(j == 0)
    def _():
        acc_scr[...] = jnp.zeros_like(acc_scr)
        wt1[...] = jnp.zeros_like(wt1)
        act0[...] = _dot(u0_ref[...], h3t_ref[...])

    @pl.when(j % 2 == 0)
    def _():
        _pipeline_stages(h3t_ref, acc_scr, stats, u_ref=u_ref, act_out=act1, act_in=act0, key0=0,
                         wt_out=wt0, wt_in=wt1, vt_ref=vt_ref)

    @pl.when(j % 2 == 1)
    def _():
        _pipeline_stages(h3t_ref, acc_scr, stats, u_ref=u_ref, act_out=act0, act_in=act1, key0=KEYS_PER_TILE,
                         wt_out=wt1, wt_in=wt0, vt_ref=vt_ref)

    @pl.when(j == pl.num_programs(1) - 1)
    def _():
        y_ref[...] = x2_ref[...] + (acc_scr[...] + _dot(vtl_ref[...], wt1[...])).T


def _peer_dense(h3t, u_bf, vt_bf, r2, e2, n1, c1, x2):
    m = x2.shape[0]
    tm, te = PEER_TM, PEER_TE
    last = N_TILES - 1
    once = pl.Buffered(1)
    key2_shape = (PEER_HEADS, N_KEYS // BF16_ROWS, BF16_ROWS, m)
    r2, e2 = r2.reshape(key2_shape), e2.reshape(key2_shape)
    key2_spec = pl.BlockSpec(key2_shape[:3] + (tm,), lambda i, j: (0, 0, 0, i), pipeline_mode=once)
    key1_spec = pl.BlockSpec((PEER_HEADS, 2 * KEYS_PER_TILE, tm), lambda i, j: (0, j // 2, i))
    u_blk, vt_blk = (te, D_MODEL), (D_MODEL, te)
    return pl.pallas_call(
        _peer_dense_kernel,
        grid=(m // tm, N_TILES),
        in_specs=[
            pl.BlockSpec((D_MODEL, tm), lambda i, j: (0, i)),
            pl.BlockSpec(u_blk, lambda i, j: (0, 0), pipeline_mode=once),
            pl.BlockSpec(u_blk, lambda i, j: (jnp.minimum(j + 1, last), 0)),
            pl.BlockSpec(vt_blk, lambda i, j: (0, jnp.maximum(j - 1, 0))),
            pl.BlockSpec(vt_blk, lambda i, j: (0, last), pipeline_mode=once),
            key2_spec, key2_spec, key1_spec, key1_spec,
            pl.BlockSpec((tm, D_MODEL), lambda i, j: (i, 0), pipeline_mode=once),
        ],
        out_specs=pl.BlockSpec((tm, D_MODEL), lambda i, j: (i, 0)),
        out_shape=jax.ShapeDtypeStruct((m, D_MODEL), F32),
        scratch_shapes=[
            pltpu.VMEM((D_MODEL, tm), F32),
            pltpu.VMEM((te, tm), F32),
            pltpu.VMEM((te, tm), F32),
            pltpu.VMEM((te, tm), BF16),
            pltpu.VMEM((te, tm), BF16),
        ],
        compiler_params=_cparams(("arbitrary", "arbitrary")),
        name="peer_dense",
    )(h3t, u_bf, u_bf, vt_bf, vt_bf, r2, e2, n1, c1, x2)


def _peer(x2, nw, wq_bf, k1_bf, k2_bf, u_bf=None, vt_bf=None, tables=None):
    h3t, r2, e2, n1, c1, *made = _peer_query(x2, nw, wq_bf, k1_bf, k2_bf, tables)
    if tables is not None:
        u_bf, vt_bf = made
    return _peer_dense(h3t, u_bf, vt_bf, r2, e2, n1, c1, x2), u_bf, vt_bf


SAMPLE_BB = 1
SAMPLE_ROWS = SAMPLE_BB * DEC_SEQ
SAMPLE_Q = HEADS_PER_GROUP * SAMPLE_ROWS
NEW_PAD = 128
NEW_ROWS = DEC_SEQ * HEADS_PER_GROUP
SAMPLE_NKC = tuple(SAMPLE_BB * HEADS_PER_GROUP * n for n in (WINDOWS[0], WINDOWS[1], DEC_SEQ * N_BACK))
ROWS_PER_CHUNK = DILATIONS[2] * HEADS_PER_GROUP


def _sample_bias(g):
    nkc = SAMPLE_NKC[g]
    r = np.arange(SAMPLE_Q)[:, None]
    rh, rb, rt = r // SAMPLE_ROWS, (r % SAMPLE_ROWS) // DEC_SEQ, r % DEC_SEQ
    c = np.arange(nkc)[None, :]
    per_batch = nkc // SAMPLE_BB
    cb, ch = c // per_batch, c % HEADS_PER_GROUP
    if g == 0:
        ok_pos = (c % per_batch) // HEADS_PER_GROUP >= rt
    elif g == 1:
        ok_pos = ((c % per_batch) // HEADS_PER_GROUP) % DILATIONS[1] == rt
    else:
        ok_pos = (c % (DEC_SEQ * HEADS_PER_GROUP)) // HEADS_PER_GROUP == rt
    ok_c = (cb == rb) & (ch == rh) & ok_pos
    n = np.arange(NEW_PAD)[None, :]
    nh, nb, nt = n // SAMPLE_ROWS, (n % SAMPLE_ROWS) // DEC_SEQ, n % DEC_SEQ
    ok_n = (n < SAMPLE_Q) & (nh == rh) & (nb == rb) & ((nt <= rt) if g == 0 else (nt == rt))
    return np.where(np.concatenate([ok_c, ok_n], axis=1), 0.0, NEG_INF).astype(np.float32)


def _sample_attn_kernel(qkv_ref, new_ref, k0_ref, v0_ref, k1_ref, v1_ref, k2_ref, v2_ref, b0_ref, b1_ref, b2_ref,
                        o_ref, lse_ref, ok0_ref, ov0_ref, ok1_ref, ov1_ref, ok2_ref, ov2_ref):
    olds = (k0_ref, v0_ref, k1_ref, v1_ref, k2_ref, v2_ref)
    outs = (ok0_ref, ov0_ref, ok1_ref, ov1_ref, ok2_ref, ov2_ref)
    biases = (b0_ref, b1_ref, b2_ref)

    for c in range(2 * N_GROUPS):
        new = new_ref[0, c * NEW_ROWS:(c + 1) * NEW_ROWS, :]
        if c < 4:
            keep = olds[c].shape[1] - NEW_ROWS
            outs[c][0, 0:keep, :] = olds[c][0, NEW_ROWS:, :]
            outs[c][0, keep:, :] = new
        else:
            keep = ROWS_PER_CHUNK - NEW_ROWS
            last = olds[c].shape[0] - 1
            outs[c][:, 0:keep, :] = olds[c][:, NEW_ROWS:, :]
            outs[c][0:last, keep:, :] = olds[c][1:, 0:NEW_ROWS, :]
            outs[c][last, keep:, :] = new

    pad = jnp.zeros((NEW_PAD - SAMPLE_Q, HEAD_DIM), F32)
    for g in range(N_GROUPS):
        rows = slice(g * SAMPLE_Q, (g + 1) * SAMPLE_Q)
        q, kn, vn = (qkv_ref[0, w * N_GROUPS * SAMPLE_Q + g * SAMPLE_Q:w * N_GROUPS * SAMPLE_Q + (g + 1) * SAMPLE_Q, :]
                     for w in range(3))
        if g < 2:
            kc, vc = olds[2 * g][0], olds[2 * g + 1][0]
        else:
            kc = olds[4][:, 0:NEW_ROWS, :].reshape(SAMPLE_NKC[2], HEAD_DIM)
            vc = olds[5][:, 0:NEW_ROWS, :].reshape(SAMPLE_NKC[2], HEAD_DIM)
        k = jnp.concatenate([kc, kn, pad], axis=0).astype(BF16)
        v = jnp.concatenate([vc, vn, pad], axis=0).astype(BF16)
        s = _dot_nt(q.astype(BF16), k) * SCALE + biases[g][...]
        o, lse = _softmax_pv(s, v)
        o_ref[0, rows, :] = o
        lse_ref[0, rows, :] = jnp.broadcast_to(lse, (SAMPLE_Q, HEAD_DIM))


def _sample_attn(qkv_hm, caches):
    nbatch = qkv_hm.shape[1] // DEC_SEQ
    per_batch = qkv_hm.reshape(N_QKV_HEADS, nbatch, DEC_SEQ, HEAD_DIM).transpose(1, 0, 2, 3)
    new = per_batch[:, N_ATTN_HEADS:].reshape(nbatch, 2, N_GROUPS, HEADS_PER_GROUP, DEC_SEQ, HEAD_DIM)
    new = new.transpose(0, 2, 1, 4, 3, 5).reshape(nbatch, 2 * N_GROUPS * NEW_ROWS, HEAD_DIM)
    args = [per_batch.reshape(nbatch, N_QKV_HEADS * DEC_SEQ, HEAD_DIM), new]
    specs = [pl.BlockSpec((1,) + a.shape[1:], lambda i: (i, 0, 0)) for a in args]
    cache_specs, cache_shapes = [], []
    for g in range(N_GROUPS):
        for cache in caches[2 * g:2 * g + 2]:
            rows = cache.shape[1] * HEADS_PER_GROUP
            if g < 2:
                view = cache.reshape(nbatch, rows, HEAD_DIM)
                cache_specs.append(pl.BlockSpec((1, rows, HEAD_DIM), lambda i: (i, 0, 0)))
            else:
                view = cache.reshape(nbatch * rows // ROWS_PER_CHUNK, ROWS_PER_CHUNK, HEAD_DIM)
                cache_specs.append(pl.BlockSpec((rows // ROWS_PER_CHUNK, ROWS_PER_CHUNK, HEAD_DIM), lambda i: (i, 0, 0)))
            args.append(view)
            cache_shapes.append(jax.ShapeDtypeStruct(view.shape, F32))
    specs += cache_specs
    for g in range(N_GROUPS):
        bias = jnp.asarray(_sample_bias(g))
        args.append(bias)
        specs.append(pl.BlockSpec(bias.shape, lambda i: (0, 0)))
    out_spec = pl.BlockSpec((1, N_GROUPS * SAMPLE_Q, HEAD_DIM), lambda i: (i, 0, 0))
    out_shape = jax.ShapeDtypeStruct((nbatch, N_GROUPS * SAMPLE_Q, HEAD_DIM), F32)
    o, lse, *slid = pl.pallas_call(
        _sample_attn_kernel,
        grid=(nbatch,),
        in_specs=specs,
        out_specs=[out_spec, out_spec] + cache_specs,
        out_shape=[out_shape, out_shape] + cache_shapes,
        compiler_params=_cparams(("arbitrary",)),
        name="sample_attn",
    )(*args)
    head_major = lambda a: a.reshape(nbatch, N_ATTN_HEADS, DEC_SEQ, HEAD_DIM).transpose(1, 0, 2, 3).reshape(
        N_ATTN_HEADS, nbatch * DEC_SEQ, HEAD_DIM)
    return head_major(o), head_major(lse), [s_.reshape(c.shape) for s_, c in zip(slid, caches)]


def kernel(x_prompt, x_sample, mem_prompt, cache_k_w128, cache_v_w128, cache_k_w512, cache_v_w512, cache_k_w2048, cache_v_w2048, state_conv, cache_mem_k, cache_mem_v, norm_mix_w, w_in, q_norm_w, k_norm_w, conv_w, conv_b, conv_ln_w, conv_ln_b, w_out, norm_mem_w, mem_norm_w, wq_mem, wk_mem, wv_mem, qn_mem_w, kn_mem_w, wo_mem, norm_ffn_w, w_peer_q, peer_keys1, peer_keys2, peer_u, peer_v):
    s = x_prompt.shape[1]
    nbatch, t_new = x_sample.shape[:2]
    ms = nbatch * t_new

    row = lambda w: w.reshape(1, -1)
    w_in_bf = w_in.astype(BF16)
    w_out_bf = w_out.astype(BF16)
    wq_mem_bf, wk_mem_bf, wv_mem_bf, wo_mem_bf = (w.astype(BF16) for w in (wq_mem, wk_mem, wv_mem, wo_mem))
    wq_peer_bf = w_peer_q.astype(BF16)
    k1_bf = peer_keys1.astype(BF16)
    k2_bf = peer_keys2.astype(BF16)
    peer_w = (row(norm_ffn_w), wq_peer_bf, k1_bf, k2_bf)
    conv_vecs = (row(conv_b), row(conv_ln_w), row(conv_ln_b))

    xp = x_prompt.reshape(s, D_MODEL)
    cos_p, sin_p = _rope_tables(jnp.arange(s, dtype=jnp.int32))
    qkv_p, u_p = _project(xp, row(norm_mix_w), w_in_bf, row(q_norm_w), row(k_norm_w), cos_p, sin_p, tm=1024)
    attn = [_prompt_attn(qkv_p, g) for g in range(N_GROUPS)]
    conv_p = _prompt_conv(u_p, conv_w, *conv_vecs)
    mem_k, mem_v = _memory_kv(mem_prompt.reshape(MEM_TOKENS, D_MODEL), row(mem_norm_w), wk_mem_bf, wv_mem_bf,
                              row(kn_mem_w))
    x2_p = _mix_mem_prompt(xp, [a[0] for a in attn], [a[1] for a in attn], conv_p, w_out_bf,
                           row(norm_mem_w), wq_mem_bf, row(qn_mem_w), mem_k, mem_v, wo_mem_bf, tm=512)
    y_p, u_bf, vt_bf = _peer(x2_p, *peer_w, tables=(peer_u, peer_v))

    def group_heads(qkv_hm, which, g):
        h0 = which * N_ATTN_HEADS + g * HEADS_PER_GROUP
        return qkv_hm[h0:h0 + HEADS_PER_GROUP]

    p_win = []
    for g, win in enumerate(WINDOWS):
        keep = min(win, s)
        for which in (1, 2):
            heads = group_heads(qkv_p, which, g)[:, s - keep:]
            p_win.append(heads.transpose(1, 0, 2)[None])
    p_conv = u_p[s - (CONV_TAPS - 1):].reshape(1, CONV_TAPS - 1, CONV_CH)
    kv4 = lambda a: a.reshape(1, MEM_TOKENS, MEM_HEADS, HEAD_DIM)

    xs = x_sample.reshape(ms, D_MODEL)
    cos_s, sin_s = _rope_tables(PAST_LEN + jnp.arange(ms, dtype=jnp.int32) % t_new)
    qkv_s, u_s = _project(xs, row(norm_mix_w), w_in_bf, row(q_norm_w), row(k_norm_w), cos_s, sin_s, tm=ms)
    caches = (cache_k_w128, cache_v_w128, cache_k_w512, cache_v_w512, cache_k_w2048, cache_v_w2048)
    o_s, lse_s, s_win = _sample_attn(qkv_s, caches)

    n_state = CONV_TAPS - 1
    ucat = jnp.concatenate(
        [state_conv, u_s.reshape(nbatch, t_new, CONV_CH),
         jnp.zeros((nbatch, SAMPLE_CONV_ROWS - n_state - t_new, CONV_CH), F32)], axis=1)
    wsh = jnp.stack([jnp.pad(conv_w, ((t, SAMPLE_CONV_ROWS - CONV_TAPS - t), (0, 0))) for t in range(t_new)])
    conv_s = _sample_conv(ucat, wsh, *conv_vecs)
    conv_s = conv_s.transpose(1, 0, 2).reshape(ms, CONV_CH)
    s_conv = ucat[:, t_new:t_new + n_state]
    x1_s = _out_proj(xs, [o_s] * N_GROUPS, [lse_s] * N_GROUPS, (0, 1, 2), conv_s, w_out_bf, tm=ms)
    x2_s = _mem_attn_sample(x1_s, row(norm_mem_w), wq_mem_bf, row(qn_mem_w),
                            cache_mem_k.reshape(-1, HEAD_DIM), cache_mem_v.reshape(-1, HEAD_DIM), wo_mem_bf)
    y_s = _peer(x2_s, *peer_w, u_bf, vt_bf)[0]

    return (y_p.reshape(x_prompt.shape), y_s.reshape(x_sample.shape), *p_win, p_conv, kv4(mem_k), kv4(mem_v),
            *s_win, s_conv)
```

```python
import functools

import numpy as np
import jax
import jax.numpy as jnp
from jax import lax
from jax.experimental import pallas as pl
from jax.experimental.pallas import tpu as pltpu

F32 = jnp.float32
BF16 = jnp.bfloat16

D_MODEL = 2048
HEAD_DIM = 128
HEADS_PER_GROUP = 4
GROUP_W = HEADS_PER_GROUP * HEAD_DIM
DILATIONS = (1, 4, 16)
WINDOWS = (128, 512, 2048)
N_BACK = 128
N_GROUPS = 3
ATTN_W = N_GROUPS * GROUP_W
CONV_CH = 512
CONV_TAPS = 31
QKV_W = 3 * ATTN_W
N_ATTN_HEADS = N_GROUPS * HEADS_PER_GROUP
N_QKV_HEADS = 3 * N_ATTN_HEADS
MEM_TOKENS = 256
MEM_W = 512
PEER_HEADS = 8
N_KEYS = 128
N_EXPERTS = N_KEYS * N_KEYS
TOPK = 16
PAST_LEN = 2048
DEC_SEQ = 4
ROPE_THETA = 10000.0
EPS = 1e-6
SCALE = HEAD_DIM ** -0.5
NEG_INF = float("-inf")

VMEM_LIMIT = 56 * 1024 * 1024


def _cparams(sem):
    return pltpu.CompilerParams(dimension_semantics=sem, vmem_limit_bytes=VMEM_LIMIT)


def _rms(x, w):
    return x * lax.rsqrt(jnp.mean(x * x, axis=-1, keepdims=True) + EPS) * w


def _dot(a, b):
    return jnp.dot(a, b, preferred_element_type=F32)


def _dot_nt(a, b):
    return lax.dot_general(a, b, (((1,), (1,)), ((), ())), preferred_element_type=F32)


def _softmax_pv(s, v):
    m = jnp.max(s, axis=-1, keepdims=True)
    p = jnp.exp(s - m)
    den = jnp.sum(p, axis=-1, keepdims=True)
    return _dot(p.astype(BF16), v) / den, m + jnp.log(den)


def _proj_kernel(x_ref, nw_ref, w_ref, wa_ref, wg_ref, qn_ref, kn_ref, cos_ref, sin_ref,
                 qkv_ref, u_ref, h_scr):
    j = pl.program_id(1)

    @pl.when(j == 0)
    def _():
        h_scr[...] = _rms(x_ref[...], nw_ref[...]).astype(BF16)
        a = _dot(h_scr[...], wa_ref[...])
        g = _dot(h_scr[...], wg_ref[...])
        u_ref[...] = a * jax.nn.sigmoid(g)

    z = _dot(h_scr[...], w_ref[...])

    @pl.when(j < 2 * N_GROUPS)
    def _():
        w = jnp.where(j < N_GROUPS, qn_ref[...], kn_ref[...])
        cos = cos_ref[...]
        sin = sin_ref[...]
        for h in range(HEADS_PER_GROUP):
            y = _rms(z[:, h * HEAD_DIM:(h + 1) * HEAD_DIM], w)
            qkv_ref[h] = y * cos + pltpu.roll(y, HEAD_DIM // 2, 1) * sin

    @pl.when(j >= 2 * N_GROUPS)
    def _():
        for h in range(HEADS_PER_GROUP):
            qkv_ref[h] = z[:, h * HEAD_DIM:(h + 1) * HEAD_DIM]


def _project(x, nw, w_in_bf, qn, kn, cos, sin, tm):
    m = x.shape[0]
    n_tiles = QKV_W // GROUP_W
    return pl.pallas_call(
        _proj_kernel,
        grid=(m // tm, n_tiles),
        in_specs=[
            pl.BlockSpec((tm, D_MODEL), lambda i, j: (i, 0)),
            pl.BlockSpec((1, D_MODEL), lambda i, j: (0, 0)),
            pl.BlockSpec((D_MODEL, GROUP_W), lambda i, j: (0, j)),
            pl.BlockSpec((D_MODEL, GROUP_W), lambda i, j: (0, n_tiles)),
            pl.BlockSpec((D_MODEL, GROUP_W), lambda i, j: (0, n_tiles + 1)),
            pl.BlockSpec((1, HEAD_DIM), lambda i, j: (0, 0)),
            pl.BlockSpec((1, HEAD_DIM), lambda i, j: (0, 0)),
            pl.BlockSpec((tm, HEAD_DIM), lambda i, j: (i, 0)),
            pl.BlockSpec((tm, HEAD_DIM), lambda i, j: (i, 0)),
        ],
        out_specs=[
            pl.BlockSpec((HEADS_PER_GROUP, tm, HEAD_DIM), lambda i, j: (j, i, 0)),
            pl.BlockSpec((tm, GROUP_W), lambda i, j: (i, 0)),
        ],
        out_shape=[
            jax.ShapeDtypeStruct((N_QKV_HEADS, m, HEAD_DIM), F32),
            jax.ShapeDtypeStruct((m, CONV_CH), F32),
        ],
        scratch_shapes=[pltpu.VMEM((tm, D_MODEL), BF16)],
        compiler_params=_cparams(("arbitrary", "arbitrary")),
        name="proj",
    )(x, nw, w_in_bf, w_in_bf, w_in_bf, qn, kn, cos, sin)


def _rope_tables(pos):
    half = HEAD_DIM // 2
    inv = ROPE_THETA ** (-jnp.arange(half, dtype=F32) / half)
    ang = pos.astype(F32)[:, None] * inv[None, :]
    cos = jnp.cos(ang)
    sin = jnp.sin(ang)
    return jnp.concatenate([cos, cos], axis=-1), jnp.concatenate([-sin, sin], axis=-1)


QUERY_BLOCK = 128


def _prompt_attn_kernel(q_ref, k_ref, v_ref, o_ref, lse_ref, kprev, vprev, *, dil):
    i = pl.program_id(0)
    qb = QUERY_BLOCK

    @pl.when(i == 0)
    def _():
        kprev[...] = jnp.zeros_like(kprev)
        vprev[...] = jnp.zeros_like(vprev)

    qi = lax.broadcasted_iota(jnp.int32, (qb, 2 * qb), 0)
    kj = lax.broadcasted_iota(jnp.int32, (qb, 2 * qb), 1)
    valid = (kj >= qi + (qb - N_BACK)) & (kj <= qi + qb) & ((kj >= qb) | (i > 0))
    for h in range(HEADS_PER_GROUP):
        for r in range(dil):
            rows = pl.ds(r, qb, stride=dil) if dil > 1 else pl.ds(0, qb)
            q = q_ref[h, rows, :].astype(BF16)
            k = jnp.concatenate([kprev[h, rows, :], k_ref[h, rows, :]], axis=0).astype(BF16)
            v = jnp.concatenate([vprev[h, rows, :], v_ref[h, rows, :]], axis=0).astype(BF16)
            o, lse = _softmax_pv(jnp.where(valid, _dot_nt(q, k) * SCALE, NEG_INF), v)
            o_ref[h, rows, :] = o
            lse_ref[h, rows, :] = jnp.broadcast_to(lse, (qb, HEAD_DIM))
    kprev[...] = k_ref[...]
    vprev[...] = v_ref[...]


def _prompt_attn(qkv_hm, g):
    s = qkv_hm.shape[1]
    dil = DILATIONS[g]
    sb = QUERY_BLOCK * dil
    blk = (HEADS_PER_GROUP, sb, HEAD_DIM)
    out_spec = pl.BlockSpec(blk, lambda i: (0, i, 0))
    out_shape = jax.ShapeDtypeStruct((HEADS_PER_GROUP, s, HEAD_DIM), F32)
    return pl.pallas_call(
        functools.partial(_prompt_attn_kernel, dil=dil),
        grid=(s // sb,),
        in_specs=[
            pl.BlockSpec(blk, lambda i: (g, i, 0)),
            pl.BlockSpec(blk, lambda i: (N_GROUPS + g, i, 0)),
            pl.BlockSpec(blk, lambda i: (2 * N_GROUPS + g, i, 0)),
        ],
        out_specs=[out_spec, out_spec],
        out_shape=[out_shape, out_shape],
        scratch_shapes=[pltpu.VMEM(blk, F32), pltpu.VMEM(blk, F32)],
        compiler_params=_cparams(("arbitrary",)),
        name=f"prompt_attn_g{g}",
    )(qkv_hm, qkv_hm, qkv_hm)


def _ln_swish(y, lnw, lnb):
    mu = jnp.mean(y, axis=-1, keepdims=True)
    yc = y - mu
    yn = yc * lax.rsqrt(jnp.mean(yc * yc, axis=-1, keepdims=True) + EPS) * lnw + lnb
    return yn * jax.nn.sigmoid(yn)


HALO = 32


SUBLANES = 8


def _prompt_conv_kernel(u_ref, halo_ref, cw_ref, cb_ref, lnw_ref, lnb_ref, o_ref, scr, *shifted):
    i = pl.program_id(0)
    tm = u_ref.shape[0]
    scr[0:HALO, :] = jnp.where(i > 0, halo_ref[...], 0.0)
    scr[HALO:, :] = u_ref[...]
    first = HALO - (CONV_TAPS - 1)
    acc = jnp.zeros((tm, CONV_CH), F32)
    for b in range(SUBLANES):
        src = scr
        if b:
            src = shifted[b - 1]
            src[...] = scr[pl.ds(b, src.shape[0]), :]
        for a in range((CONV_TAPS - 1 + first) // SUBLANES + 1):
            j = SUBLANES * a + b - first
            if 0 <= j < CONV_TAPS:
                acc = acc + cw_ref[j:j + 1, :] * src[pl.ds(SUBLANES * a, tm), :]
    o_ref[...] = _ln_swish(acc + cb_ref[...], lnw_ref[...], lnb_ref[...])


def _prompt_conv(u, cw, cb, lnw, lnb, tm=512):
    m = u.shape[0]
    vec = pl.BlockSpec((1, CONV_CH), lambda i: (0, 0))
    return pl.pallas_call(
        _prompt_conv_kernel,
        grid=(m // tm,),
        in_specs=[
            pl.BlockSpec((tm, CONV_CH), lambda i: (i, 0)),
            pl.BlockSpec((HALO, CONV_CH), lambda i: (jnp.maximum(i * (tm // HALO) - 1, 0), 0)),
            pl.BlockSpec((CONV_TAPS, CONV_CH), lambda i: (0, 0)),
            vec, vec, vec,
        ],
        out_specs=pl.BlockSpec((tm, CONV_CH), lambda i: (i, 0)),
        out_shape=jax.ShapeDtypeStruct((m, CONV_CH), F32),
        scratch_shapes=[pltpu.VMEM((tm + HALO, CONV_CH), F32)]
        + [pltpu.VMEM((tm + HALO - SUBLANES, CONV_CH), F32)] * (SUBLANES - 1),
        compiler_params=_cparams(("arbitrary",)),
        name="prompt_conv",
    )(u, u, cw, cb, lnw, lnb)


SAMPLE_CONV_ROWS = 40


def _sample_conv_kernel(ucat_ref, wsh_ref, cb_ref, lnw_ref, lnb_ref, o_ref):
    ucat = ucat_ref[...]
    for t in range(DEC_SEQ):
        y = jnp.sum(ucat * wsh_ref[t][None], axis=1)
        o_ref[t] = _ln_swish(y + cb_ref[...], lnw_ref[...], lnb_ref[...])


def _sample_conv(ucat, wsh, cb, lnw, lnb):
    nb = ucat.shape[0]
    return pl.pallas_call(
        _sample_conv_kernel,
        out_shape=jax.ShapeDtypeStruct((DEC_SEQ, nb, CONV_CH), F32),
        compiler_params=pltpu.CompilerParams(vmem_limit_bytes=VMEM_LIMIT),
        name="sample_conv",
    )(ucat, wsh, cb, lnw, lnb)


def _combine_and_project(x_ref, o_refs, l_refs, cv_ref, w_ref, a_scr):
    for h in range(HEADS_PER_GROUP):
        l0, l1, l2 = (l_ref[h] for l_ref in l_refs)
        lm = jnp.maximum(jnp.maximum(l0, l1), l2)
        es = (jnp.exp(l0 - lm), jnp.exp(l1 - lm), jnp.exp(l2 - lm))
        den = es[0] + es[1] + es[2]
        for g in range(N_GROUPS):
            c0 = g * GROUP_W + h * HEAD_DIM
            a_scr[:, c0:c0 + HEAD_DIM] = ((es[g] / den) * o_refs[g][h]).astype(BF16)
    a_scr[:, ATTN_W:] = cv_ref[...].astype(BF16)
    return x_ref[...] + _dot(a_scr[...], w_ref[...])


def _out_proj_kernel(x_ref, o0_ref, o1_ref, o2_ref, l0_ref, l1_ref, l2_ref, cv_ref, w_ref, y_ref, a_scr):
    y_ref[...] = _combine_and_project(x_ref, (o0_ref, o1_ref, o2_ref), (l0_ref, l1_ref, l2_ref), cv_ref, w_ref, a_scr)


def _out_proj(x, os_, ls_, gidx, conv, w_out_bf, tm):
    m = x.shape[0]
    gspecs = [pl.BlockSpec((HEADS_PER_GROUP, tm, HEAD_DIM), lambda i, gi=gi: (gi, i, 0)) for gi in gidx]
    return pl.pallas_call(
        _out_proj_kernel,
        grid=(m // tm,),
        in_specs=[pl.BlockSpec((tm, D_MODEL), lambda i: (i, 0))] + gspecs + gspecs
        + [pl.BlockSpec((tm, CONV_CH), lambda i: (i, 0)), pl.BlockSpec((D_MODEL, D_MODEL), lambda i: (0, 0))],
        out_specs=pl.BlockSpec((tm, D_MODEL), lambda i: (i, 0)),
        out_shape=jax.ShapeDtypeStruct((m, D_MODEL), F32),
        scratch_shapes=[pltpu.VMEM((tm, D_MODEL), BF16)],
        compiler_params=_cparams(("arbitrary",)),
        name="out_proj",
    )(x, *os_, *ls_, conv, w_out_bf)


def _memory_kv_kernel(mem_ref, nw_ref, wk_ref, wv_ref, kn_ref, k_ref, v_ref):
    h = _rms(mem_ref[...], nw_ref[...]).astype(BF16)
    k = _dot(h, wk_ref[...])
    v_ref[...] = _dot(h, wv_ref[...])
    for hd in range(MEM_W // HEAD_DIM):
        hs = slice(hd * HEAD_DIM, (hd + 1) * HEAD_DIM)
        k_ref[:, hs] = _rms(k[:, hs], kn_ref[...])


def _memory_kv(mem, nw, wk_bf, wv_bf, kn):
    return pl.pallas_call(
        _memory_kv_kernel,
        out_shape=[jax.ShapeDtypeStruct((MEM_TOKENS, MEM_W), F32)] * 2,
        compiler_params=pltpu.CompilerParams(vmem_limit_bytes=VMEM_LIMIT),
        name="memory_kv",
    )(mem, nw, wk_bf, wv_bf, kn)


MEM_HEADS = MEM_W // HEAD_DIM


def _mix_mem_prompt_kernel(x_ref, o0_ref, o1_ref, o2_ref, l0_ref, l1_ref, l2_ref, cv_ref, w_ref,
                           nw_ref, wq_ref, qn_ref, k_ref, v_ref, wo_ref, y_ref, a_scr, o_scr):
    x1 = _combine_and_project(x_ref, (o0_ref, o1_ref, o2_ref), (l0_ref, l1_ref, l2_ref), cv_ref, w_ref, a_scr)
    q = _dot(_rms(x1, nw_ref[...]).astype(BF16), wq_ref[...])
    for h in range(MEM_HEADS):
        hs = slice(h * HEAD_DIM, (h + 1) * HEAD_DIM)
        qh = _rms(q[:, hs], qn_ref[...]).astype(BF16)
        s = _dot_nt(qh, k_ref[:, hs].astype(BF16)) * SCALE
        o_scr[:, hs] = _softmax_pv(s, v_ref[:, hs].astype(BF16))[0].astype(BF16)
    y_ref[...] = x1 + _dot(o_scr[...], wo_ref[...])


def _mix_mem_prompt(x, os_, ls_, conv, w_out_bf, nw, wq_bf, qn, k, v, wo_bf, tm):
    m = x.shape[0]
    once = pl.Buffered(1)
    gspec = pl.BlockSpec((HEADS_PER_GROUP, tm, HEAD_DIM), lambda i: (0, i, 0))
    kv_spec = pl.BlockSpec((MEM_TOKENS, MEM_W), lambda i: (0, 0))
    return pl.pallas_call(
        _mix_mem_prompt_kernel,
        grid=(m // tm,),
        in_specs=[pl.BlockSpec((tm, D_MODEL), lambda i: (i, 0))] + [gspec] * 6 + [
            pl.BlockSpec((tm, CONV_CH), lambda i: (i, 0)),
            pl.BlockSpec((D_MODEL, D_MODEL), lambda i: (0, 0), pipeline_mode=once),
            pl.BlockSpec((1, D_MODEL), lambda i: (0, 0)),
            pl.BlockSpec((D_MODEL, MEM_W), lambda i: (0, 0), pipeline_mode=once),
            pl.BlockSpec((1, HEAD_DIM), lambda i: (0, 0)),
            kv_spec, kv_spec,
            pl.BlockSpec((MEM_W, D_MODEL), lambda i: (0, 0), pipeline_mode=once),
        ],
        out_specs=pl.BlockSpec((tm, D_MODEL), lambda i: (i, 0)),
        out_shape=jax.ShapeDtypeStruct((m, D_MODEL), F32),
        scratch_shapes=[pltpu.VMEM((tm, D_MODEL), BF16), pltpu.VMEM((tm, MEM_W), BF16)],
        compiler_params=_cparams(("arbitrary",)),
        name="mix_mem_prompt",
    )(x, *os_, *ls_, conv, w_out_bf, nw, wq_bf, qn, k, v, wo_bf)


MEM_BB = 4
MEM_ROWS = MEM_BB * DEC_SEQ
MEM_KEYS = MEM_BB * MEM_TOKENS * MEM_HEADS


def _mem_attn_sample_kernel(x_ref, nw_ref, wq_ref, qn_ref, k_ref, v_ref, bias_ref, wo_ref, y_ref, o_scr):
    x = x_ref[...]
    q = _dot(_rms(x, nw_ref[...]).astype(BF16), wq_ref[...])
    qs = jnp.concatenate([_rms(q[:, h * HEAD_DIM:(h + 1) * HEAD_DIM], qn_ref[...]) for h in range(MEM_HEADS)],
                         axis=0).astype(BF16)
    s = _dot_nt(qs, k_ref[...].astype(BF16)) * SCALE + bias_ref[...]
    o = _softmax_pv(s, v_ref[...].astype(BF16))[0]
    for h in range(MEM_HEADS):
        o_scr[:, h * HEAD_DIM:(h + 1) * HEAD_DIM] = o[h * MEM_ROWS:(h + 1) * MEM_ROWS].astype(BF16)
    y_ref[...] = x + _dot(o_scr[...], wo_ref[...])


def _mem_sample_bias():
    r = np.arange(MEM_HEADS * MEM_ROWS)[:, None]
    c = np.arange(MEM_KEYS)[None, :]
    ok = (r // MEM_ROWS == c % MEM_HEADS) & ((r % MEM_ROWS) // DEC_SEQ == c // (MEM_TOKENS * MEM_HEADS))
    return np.where(ok, 0.0, NEG_INF).astype(np.float32)


def _mem_attn_sample(x, nw, wq_bf, qn, k_flat, v_flat, wo_bf):
    m = x.shape[0]
    bias = jnp.asarray(_mem_sample_bias())
    kv_spec = pl.BlockSpec((MEM_KEYS, HEAD_DIM), lambda i: (i, 0))
    return pl.pallas_call(
        _mem_attn_sample_kernel,
        grid=(m // MEM_ROWS,),
        in_specs=[
            pl.BlockSpec((MEM_ROWS, D_MODEL), lambda i: (i, 0)),
            pl.BlockSpec((1, D_MODEL), lambda i: (0, 0)),
            pl.BlockSpec((D_MODEL, MEM_W), lambda i: (0, 0)),
            pl.BlockSpec((1, HEAD_DIM), lambda i: (0, 0)),
            kv_spec, kv_spec,
            pl.BlockSpec(bias.shape, lambda i: (0, 0)),
            pl.BlockSpec((MEM_W, D_MODEL), lambda i: (0, 0)),
        ],
        out_specs=pl.BlockSpec((MEM_ROWS, D_MODEL), lambda i: (i, 0)),
        out_shape=jax.ShapeDtypeStruct((m, D_MODEL), F32),
        scratch_shapes=[pltpu.VMEM((MEM_ROWS, MEM_W), BF16)],
        compiler_params=_cparams(("arbitrary",)),
        name="mem_attn_sample",
    )(x, nw, wq_bf, qn, k_flat, v_flat, bias, wo_bf)


PEER_TQ = 128
NOT_TOP = 99.0
CAND_ROWS = 16 + 7 * 8 + 8
INVALID_FLAT = 999.0


def _cand_flat_index():
    flat = np.full((CAND_ROWS,), INVALID_FLAT, np.float32)
    flat[0:16] = np.arange(16)
    for a in range(1, 8):
        nb = TOPK // (a + 1)
        flat[16 + 8 * (a - 1):16 + 8 * (a - 1) + nb] = a * TOPK + np.arange(nb)
    flat[72:80] = np.arange(8, 16) * TOPK
    return np.broadcast_to(flat[:, None], (CAND_ROWS, PEER_TQ)).copy()


def _topk_rows(sa, sb, row, va_scr, vb_scr, ia_scr, ib_scr):
    for a in range(TOPK):
        ma = jnp.max(sa, axis=0, keepdims=True)
        mb = jnp.max(sb, axis=0, keepdims=True)
        ia = jnp.min(jnp.where(sa == ma, row, float(N_KEYS)), axis=0, keepdims=True)
        ib = jnp.min(jnp.where(sb == mb, row, float(N_KEYS)), axis=0, keepdims=True)
        sa = jnp.where(row == ia, NEG_INF, sa)
        sb = jnp.where(row == ib, NEG_INF, sb)
        va_scr[a:a + 1, :] = ma
        vb_scr[a:a + 1, :] = mb
        ia_scr[a:a + 1, :] = ia
        ib_scr[a:a + 1, :] = ib


def _peer_query_kernel(x_ref, nw_ref, wq_ref, k1_ref, k2_ref, flat_ref, *rest, with_tables):
    if with_tables:
        u_ref, v_ref, *rest = rest
        (h3t_ref, r2_ref, e2_ref, n1_ref, c1_ref, u_bf_ref, vt_bf_ref,
         q_scr, v1_scr, v2_scr, i1_scr, i2_scr) = rest
        u_bf_ref[...] = u_ref[...].astype(BF16)
        vt_bf_ref[...] = v_ref[...].T.astype(BF16)
    else:
        h3t_ref, r2_ref, e2_ref, n1_ref, c1_ref, q_scr, v1_scr, v2_scr, i1_scr, i2_scr = rest
    h3 = _rms(x_ref[...], nw_ref[...])
    h3t_ref[...] = h3.T.astype(BF16)
    q_scr[...] = _dot(h3.astype(BF16), wq_ref[...]).astype(BF16)
    row = lax.broadcasted_iota(jnp.int32, (N_KEYS, PEER_TQ), 0).astype(F32)
    flat = flat_ref[...]
    cand_ok = flat < INVALID_FLAT

    def head(h, carry):
        off = pl.multiple_of(h * 2 * HEAD_DIM, 2 * HEAD_DIM)
        s1 = _dot_nt(k1_ref[h], q_scr[:, pl.ds(off, HEAD_DIM)])
        s2 = _dot_nt(k2_ref[h], q_scr[:, pl.ds(off + HEAD_DIM, HEAD_DIM)])
        _topk_rows(s1, s2, row, v1_scr, v2_scr, i1_scr, i2_scr)
        pieces = [v1_scr[0:1, :] + v2_scr[...]]
        for a in range(1, 8):
            pieces.append(v1_scr[a:a + 1, :] + v2_scr[0:8, :])
        pieces.append(v1_scr[8:16, :] + v2_scr[0:1, :])
        cand0 = jnp.where(cand_ok, jnp.concatenate(pieces, axis=0), NEG_INF)
        cand = cand0
        for _ in range(TOPK):
            m = jnp.max(cand, axis=0, keepdims=True)
            f = jnp.min(jnp.where(cand == m, flat, 2 * INVALID_FLAT), axis=0, keepdims=True)
            cand = jnp.where(flat == f, NEG_INF, cand)
        sel = cand_ok & (cand == NEG_INF)
        z = jnp.sum(jnp.where(sel, jnp.exp(cand0 - cand0[0:1, :]), 0.0), axis=0, keepdims=True)
        self_ = sel.astype(F32)
        n1 = jnp.zeros((N_KEYS, PEER_TQ), F32)
        for a in range(TOPK):
            if a == 0:
                na = jnp.sum(self_[0:16, :], axis=0, keepdims=True)
            elif a < 8:
                na = jnp.sum(self_[16 + 8 * (a - 1):16 + 8 * a, :], axis=0, keepdims=True)
            else:
                na = self_[72 + a - 8:72 + a - 7, :]
            n1 = jnp.where(row == i1_scr[a:a + 1, :], na, n1)
        rank2 = jnp.full((N_KEYS, PEER_TQ), NOT_TOP, F32)
        for a in range(TOPK):
            rank2 = jnp.where(row == i2_scr[a:a + 1, :], float(a), rank2)
        r2_ref[h] = rank2.astype(BF16)
        e2_ref[h] = jnp.exp(s2 - v2_scr[0:1, :]).astype(BF16)
        n1_ref[h] = n1
        c1_ref[h] = jnp.exp(s1 - v1_scr[0:1, :]) / z
        return carry

    lax.fori_loop(0, PEER_HEADS, head, 0)


def _peer_query(x2, nw, wq_bf, k1_bf, k2_bf, tables=None):
    m = x2.shape[0]
    tq = PEER_TQ
    steps = m // tq
    flat = jnp.asarray(_cand_flat_index())
    key_spec = pl.BlockSpec((PEER_HEADS, N_KEYS, HEAD_DIM), lambda i: (0, 0, 0))
    stat_spec = pl.BlockSpec((PEER_HEADS, N_KEYS, tq), lambda i: (0, 0, i))
    stat_shape = lambda dt: jax.ShapeDtypeStruct((PEER_HEADS, N_KEYS, m), dt)
    table_in, table_out, table_shapes = [], [], []
    if tables is not None:
        rows = N_EXPERTS // steps
        table_in = [pl.BlockSpec((rows, D_MODEL), lambda i: (i, 0))] * 2
        table_out = [pl.BlockSpec((rows, D_MODEL), lambda i: (i, 0)), pl.BlockSpec((D_MODEL, rows), lambda i: (0, i))]
        table_shapes = [jax.ShapeDtypeStruct((N_EXPERTS, D_MODEL), BF16), jax.ShapeDtypeStruct((D_MODEL, N_EXPERTS), BF16)]
    return pl.pallas_call(
        functools.partial(_peer_query_kernel, with_tables=tables is not None),
        grid=(steps,),
        in_specs=[
            pl.BlockSpec((tq, D_MODEL), lambda i: (i, 0)),
            pl.BlockSpec((1, D_MODEL), lambda i: (0, 0)),
            pl.BlockSpec((D_MODEL, 2 * HEAD_DIM * PEER_HEADS), lambda i: (0, 0)),
            key_spec, key_spec,
            pl.BlockSpec((CAND_ROWS, tq), lambda i: (0, 0)),
        ] + table_in,
        out_specs=[pl.BlockSpec((D_MODEL, tq), lambda i: (0, i))] + [stat_spec] * 4 + table_out,
        out_shape=[jax.ShapeDtypeStruct((D_MODEL, m), BF16)] + [stat_shape(BF16)] * 2 + [stat_shape(F32)] * 2
        + table_shapes,
        scratch_shapes=[
            pltpu.VMEM((tq, 2 * HEAD_DIM * PEER_HEADS), BF16),
            pltpu.VMEM((TOPK, tq), F32),
            pltpu.VMEM((TOPK, tq), F32),
            pltpu.VMEM((TOPK, tq), F32),
            pltpu.VMEM((TOPK, tq), F32),
        ],
        compiler_params=_cparams(("arbitrary",)),
        name="peer_query",
    )(x2, nw, wq_bf, k1_bf, k2_bf, flat, *(tables or ()))


PEER_TM = 512
PEER_TE = 512
KEYS_PER_TILE = PEER_TE // N_KEYS
N_TILES = N_EXPERTS // PEER_TE
LANE = 128
BF16_ROWS = 16
INV_SQRT2 = 0.7071067811865476


def _gate_lanes(act_ref, wt_ref, r2_ref, e2_ref, n1_ref, c1_ref, key0, lc):
    ls = slice(lc * LANE, (lc + 1) * LANE)
    shape3 = (N_KEYS // BF16_ROWS, BF16_ROWS, LANE)
    gates = [jnp.zeros(shape3, BF16) for _ in range(KEYS_PER_TILE)]
    for h in range(PEER_HEADS):
        r2 = r2_ref[h, :, :, ls]
        e2 = e2_ref[h, :, :, ls]
        for c in range(KEYS_PER_TILE):
            n1 = jnp.broadcast_to(n1_ref[h, key0 + c:key0 + c + 1, ls], shape3[1:]).astype(BF16)
            c1 = jnp.broadcast_to(c1_ref[h, key0 + c:key0 + c + 1, ls], shape3[1:]).astype(BF16)
            gates[c] = gates[c] + (jnp.clip(n1 - r2, 0.0, 1.0) * e2) * c1
    for c in range(KEYS_PER_TILE):
        a = act_ref[c * N_KEYS:(c + 1) * N_KEYS, ls]
        gelu = 0.5 * a * (1.0 + lax.erf(a * INV_SQRT2))
        wt_ref[c * N_KEYS:(c + 1) * N_KEYS, ls] = (gates[c] * gelu.astype(BF16).reshape(shape3)).reshape(N_KEYS, LANE)


ACT_HALF = PEER_TM // 2
OUT_ROWS = D_MODEL // 4
assert D_MODEL // OUT_ROWS == PEER_TM // LANE


def _pipeline_stages(h3t_ref, acc_scr, stats, *, u_ref, act_out, act_in, key0, wt_out, wt_in, vt_ref):
    def act_piece(n):
        ls = slice(n * ACT_HALF, (n + 1) * ACT_HALF)
        act_out[:, ls] = _dot(u_ref[...], h3t_ref[:, ls])

    def out_piece(r):
        rs = slice(r * OUT_ROWS, (r + 1) * OUT_ROWS)
        acc_scr[rs, :] += _dot(vt_ref[rs, :], wt_in[...])

    def gate_piece(lc):
        _gate_lanes(act_in, wt_out, *stats, key0, lc)

    for r in range(D_MODEL // OUT_ROWS):
        out_piece(r)
        gate_piece(r)
    act_piece(0)
    act_piece(1)


def _peer_dense_kernel(h3t_ref, u0_ref, u_ref, vt_ref, vtl_ref, r2_ref, e2_ref, n1_ref, c1_ref,
                       x2_ref, y_ref, acc_scr, act0, act1, wt0, wt1):
    j = pl.program_id(1)
    stats = (r2_ref, e2_ref, n1_ref, c1_ref)

    @pl.when(j == 0)
    def _():
        acc_scr[...] = jnp.zeros_like(acc_scr)
        wt1[...] = jnp.zeros_like(wt1)
        act0[...] = _dot(u0_ref[...], h3t_ref[...])

    @pl.when(j % 2 == 0)
    def _():
        _pipeline_stages(h3t_ref, acc_scr, stats, u_ref=u_ref, act_out=act1, act_in=act0, key0=0,
                         wt_out=wt0, wt_in=wt1, vt_ref=vt_ref)

    @pl.when(j % 2 == 1)
    def _():
        _pipeline_stages(h3t_ref, acc_scr, stats, u_ref=u_ref, act_out=act0, act_in=act1, key0=KEYS_PER_TILE,
                         wt_out=wt1, wt_in=wt0, vt_ref=vt_ref)

    @pl.when(j == pl.num_programs(1) - 1)
    def _():
        y_ref[...] = x2_ref[...] + (acc_scr[...] + _dot(vtl_ref[...], wt1[...])).T


def _peer_dense(h3t, u_bf, vt_bf, r2, e2, n1, c1, x2):
    m = x2.shape[0]
    tm, te = PEER_TM, PEER_TE
    last = N_TILES - 1
    once = pl.Buffered(1)
    key2_shape = (PEER_HEADS, N_KEYS // BF16_ROWS, BF16_ROWS, m)
    r2, e2 = r2.reshape(key2_shape), e2.reshape(key2_shape)
    key2_spec = pl.BlockSpec(key2_shape[:3] + (tm,), lambda i, j: (0, 0, 0, i), pipeline_mode=once)
    key1_spec = pl.BlockSpec((PEER_HEADS, 2 * KEYS_PER_TILE, tm), lambda i, j: (0, j // 2, i))
    u_blk, vt_blk = (te, D_MODEL), (D_MODEL, te)
    return pl.pallas_call(
        _peer_dense_kernel,
        grid=(m // tm, N_TILES),
        in_specs=[
            pl.BlockSpec((D_MODEL, tm), lambda i, j: (0, i)),
            pl.BlockSpec(u_blk, lambda i, j: (0, 0), pipeline_mode=once),
            pl.BlockSpec(u_blk, lambda i, j: (jnp.minimum(j + 1, last), 0)),
            pl.BlockSpec(vt_blk, lambda i, j: (0, jnp.maximum(j - 1, 0))),
            pl.BlockSpec(vt_blk, lambda i, j: (0, last), pipeline_mode=once),
            key2_spec, key2_spec, key1_spec, key1_spec,
            pl.BlockSpec((tm, D_MODEL), lambda i, j: (i, 0), pipeline_mode=once),
        ],
        out_specs=pl.BlockSpec((tm, D_MODEL), lambda i, j: (i, 0)),
        out_shape=jax.ShapeDtypeStruct((m, D_MODEL), F32),
        scratch_shapes=[
            pltpu.VMEM((D_MODEL, tm), F32),
            pltpu.VMEM((te, tm), F32),
            pltpu.VMEM((te, tm), F32),
            pltpu.VMEM((te, tm), BF16),
            pltpu.VMEM((te, tm), BF16),
        ],
        compiler_params=_cparams(("arbitrary", "arbitrary")),
        name="peer_dense",
    )(h3t, u_bf, u_bf, vt_bf, vt_bf, r2, e2, n1, c1, x2)


def _peer(x2, nw, wq_bf, k1_bf, k2_bf, u_bf=None, vt_bf=None, tables=None):
    h3t, r2, e2, n1, c1, *made = _peer_query(x2, nw, wq_bf, k1_bf, k2_bf, tables)
    if tables is not None:
        u_bf, vt_bf = made
    return _peer_dense(h3t, u_bf, vt_bf, r2, e2, n1, c1, x2), u_bf, vt_bf


SAMPLE_BB = 1
SAMPLE_ROWS = SAMPLE_BB * DEC_SEQ
SAMPLE_Q = HEADS_PER_GROUP * SAMPLE_ROWS
NEW_PAD = 128
NEW_ROWS = DEC_SEQ * HEADS_PER_GROUP
SAMPLE_NKC = tuple(SAMPLE_BB * HEADS_PER_GROUP * n for n in (WINDOWS[0], WINDOWS[1], DEC_SEQ * N_BACK))
ROWS_PER_CHUNK = DILATIONS[2] * HEADS_PER_GROUP


def _sample_bias(g):
    nkc = SAMPLE_NKC[g]
    r = np.arange(SAMPLE_Q)[:, None]
    rh, rb, rt = r // SAMPLE_ROWS, (r % SAMPLE_ROWS) // DEC_SEQ, r % DEC_SEQ
    c = np.arange(nkc)[None, :]
    per_batch = nkc // SAMPLE_BB
    cb, ch = c // per_batch, c % HEADS_PER_GROUP
    if g == 0:
        ok_pos = (c % per_batch) // HEADS_PER_GROUP >= rt
    elif g == 1:
        ok_pos = ((c % per_batch) // HEADS_PER_GROUP) % DILATIONS[1] == rt
    else:
        ok_pos = (c % (DEC_SEQ * HEADS_PER_GROUP)) // HEADS_PER_GROUP == rt
    ok_c = (cb == rb) & (ch == rh) & ok_pos
    n = np.arange(NEW_PAD)[None, :]
    nh, nb, nt = n // SAMPLE_ROWS, (n % SAMPLE_ROWS) // DEC_SEQ, n % DEC_SEQ
    ok_n = (n < SAMPLE_Q) & (nh == rh) & (nb == rb) & ((nt <= rt) if g == 0 else (nt == rt))
    return np.where(np.concatenate([ok_c, ok_n], axis=1), 0.0, NEG_INF).astype(np.float32)


def _sample_attn_kernel(qkv_ref, new_ref, k0_ref, v0_ref, k1_ref, v1_ref, k2_ref, v2_ref, b0_ref, b1_ref, b2_ref,
                        o_ref, lse_ref, ok0_ref, ov0_ref, ok1_ref, ov1_ref, ok2_ref, ov2_ref):
    olds = (k0_ref, v0_ref, k1_ref, v1_ref, k2_ref, v2_ref)
    outs = (ok0_ref, ov0_ref, ok1_ref, ov1_ref, ok2_ref, ov2_ref)
    biases = (b0_ref, b1_ref, b2_ref)

    for c in range(2 * N_GROUPS):
        new = new_ref[0, c * NEW_ROWS:(c + 1) * NEW_ROWS, :]
        if c < 4:
            keep = olds[c].shape[1] - NEW_ROWS
            outs[c][0, 0:keep, :] = olds[c][0, NEW_ROWS:, :]
            outs[c][0, keep:, :] = new
        else:
            keep = ROWS_PER_CHUNK - NEW_ROWS
            last = olds[c].shape[0] - 1
            outs[c][:, 0:keep, :] = olds[c][:, NEW_ROWS:, :]
            outs[c][0:last, keep:, :] = olds[c][1:, 0:NEW_ROWS, :]
            outs[c][last, keep:, :] = new

    pad = jnp.zeros((NEW_PAD - SAMPLE_Q, HEAD_DIM), F32)
    for g in range(N_GROUPS):
        rows = slice(g * SAMPLE_Q, (g + 1) * SAMPLE_Q)
        q, kn, vn = (qkv_ref[0, w * N_GROUPS * SAMPLE_Q + g * SAMPLE_Q:w * N_GROUPS * SAMPLE_Q + (g + 1) * SAMPLE_Q, :]
                     for w in range(3))
        if g < 2:
            kc, vc = olds[2 * g][0], olds[2 * g + 1][0]
        else:
            kc = olds[4][:, 0:NEW_ROWS, :].reshape(SAMPLE_NKC[2], HEAD_DIM)
            vc = olds[5][:, 0:NEW_ROWS, :].reshape(SAMPLE_NKC[2], HEAD_DIM)
        k = jnp.concatenate([kc, kn, pad], axis=0).astype(BF16)
        v = jnp.concatenate([vc, vn, pad], axis=0).astype(BF16)
        s = _dot_nt(q.astype(BF16), k) * SCALE + biases[g][...]
        o, lse = _softmax_pv(s, v)
        o_ref[0, rows, :] = o
        lse_ref[0, rows, :] = jnp.broadcast_to(lse, (SAMPLE_Q, HEAD_DIM))


def _sample_attn(qkv_hm, caches):
    nbatch = qkv_hm.shape[1] // DEC_SEQ
    per_batch = qkv_hm.reshape(N_QKV_HEADS, nbatch, DEC_SEQ, HEAD_DIM).transpose(1, 0, 2, 3)
    new = per_batch[:, N_ATTN_HEADS:].reshape(nbatch, 2, N_GROUPS, HEADS_PER_GROUP, DEC_SEQ, HEAD_DIM)
    new = new.transpose(0, 2, 1, 4, 3, 5).reshape(nbatch, 2 * N_GROUPS * NEW_ROWS, HEAD_DIM)
    args = [per_batch.reshape(nbatch, N_QKV_HEADS * DEC_SEQ, HEAD_DIM), new]
    specs = [pl.BlockSpec((1,) + a.shape[1:], lambda i: (i, 0, 0)) for a in args]
    cache_specs, cache_shapes = [], []
    for g in range(N_GROUPS):
        for cache in caches[2 * g:2 * g + 2]:
            rows = cache.shape[1] * HEADS_PER_GROUP
            if g < 2:
                view = cache.reshape(nbatch, rows, HEAD_DIM)
                cache_specs.append(pl.BlockSpec((1, rows, HEAD_DIM), lambda i: (i, 0, 0)))
            else:
                view = cache.reshape(nbatch * rows // ROWS_PER_CHUNK, ROWS_PER_CHUNK, HEAD_DIM)
                cache_specs.append(pl.BlockSpec((rows // ROWS_PER_CHUNK, ROWS_PER_CHUNK, HEAD_DIM), lambda i: (i, 0, 0)))
            args.append(view)
            cache_shapes.append(jax.ShapeDtypeStruct(view.shape, F32))
    specs += cache_specs
    for g in range(N_GROUPS):
        bias = jnp.asarray(_sample_bias(g))
        args.append(bias)
        specs.append(pl.BlockSpec(bias.shape, lambda i: (0, 0)))
    out_spec = pl.BlockSpec((1, N_GROUPS * SAMPLE_Q, HEAD_DIM), lambda i: (i, 0, 0))
    out_shape = jax.ShapeDtypeStruct((nbatch, N_GROUPS * SAMPLE_Q, HEAD_DIM), F32)
    o, lse, *slid = pl.pallas_call(
        _sample_attn_kernel,
        grid=(nbatch,),
        in_specs=specs,
        out_specs=[out_spec, out_spec] + cache_specs,
        out_shape=[out_shape, out_shape] + cache_shapes,
        compiler_params=_cparams(("arbitrary",)),
        name="sample_attn",
    )(*args)
    head_major = lambda a: a.reshape(nbatch, N_ATTN_HEADS, DEC_SEQ, HEAD_DIM).transpose(1, 0, 2, 3).reshape(
        N_ATTN_HEADS, nbatch * DEC_SEQ, HEAD_DIM)
    return head_major(o), head_major(lse), [s_.reshape(c.shape) for s_, c in zip(slid, caches)]


def kernel(x_prompt, x_sample, mem_prompt, cache_k_w128, cache_v_w128, cache_k_w512, cache_v_w512, cache_k_w2048, cache_v_w2048, state_conv, cache_mem_k, cache_mem_v, norm_mix_w, w_in, q_norm_w, k_norm_w, conv_w, conv_b, conv_ln_w, conv_ln_b, w_out, norm_mem_w, mem_norm_w, wq_mem, wk_mem, wv_mem, qn_mem_w, kn_mem_w, wo_mem, norm_ffn_w, w_peer_q, peer_keys1, peer_keys2, peer_u, peer_v):
    s = x_prompt.shape[1]
    nbatch, t_new = x_sample.shape[:2]
    ms = nbatch * t_new

    row = lambda w: w.reshape(1, -1)
    w_in_bf = w_in.astype(BF16)
    w_out_bf = w_out.astype(BF16)
    wq_mem_bf, wk_mem_bf, wv_mem_bf, wo_mem_bf = (w.astype(BF16) for w in (wq_mem, wk_mem, wv_mem, wo_mem))
    wq_peer_bf = w_peer_q.astype(BF16)
    k1_bf = peer_keys1.astype(BF16)
    k2_bf = peer_keys2.astype(BF16)
    peer_w = (row(norm_ffn_w), wq_peer_bf, k1_bf, k2_bf)
    conv_vecs = (row(conv_b), row(conv_ln_w), row(conv_ln_b))

    xp = x_prompt.reshape(s, D_MODEL)
    cos_p, sin_p = _rope_tables(jnp.arange(s, dtype=jnp.int32))
    qkv_p, u_p = _project(xp, row(norm_mix_w), w_in_bf, row(q_norm_w), row(k_norm_w), cos_p, sin_p, tm=1024)
    attn = [_prompt_attn(qkv_p, g) for g in range(N_GROUPS)]
    conv_p = _prompt_conv(u_p, conv_w, *conv_vecs)
    mem_k, mem_v = _memory_kv(mem_prompt.reshape(MEM_TOKENS, D_MODEL), row(mem_norm_w), wk_mem_bf, wv_mem_bf,
                              row(kn_mem_w))
    x2_p = _mix_mem_prompt(xp, [a[0] for a in attn], [a[1] for a in attn], conv_p, w_out_bf,
                           row(norm_mem_w), wq_mem_bf, row(qn_mem_w), mem_k, mem_v, wo_mem_bf, tm=512)
    y_p, u_bf, vt_bf = _peer(x2_p, *peer_w, tables=(peer_u, peer_v))

    def group_heads(qkv_hm, which, g):
        h0 = which * N_ATTN_HEADS + g * HEADS_PER_GROUP
        return qkv_hm[h0:h0 + HEADS_PER_GROUP]

    p_win = []
    for g, win in enumerate(WINDOWS):
        keep = min(win, s)
        for which in (1, 2):
            heads = group_heads(qkv_p, which, g)[:, s - keep:]
            p_win.append(heads.transpose(1, 0, 2)[None])
    p_conv = u_p[s - (CONV_TAPS - 1):].reshape(1, CONV_TAPS - 1, CONV_CH)
    kv4 = lambda a: a.reshape(1, MEM_TOKENS, MEM_HEADS, HEAD_DIM)

    xs = x_sample.reshape(ms, D_MODEL)
    cos_s, sin_s = _rope_tables(PAST_LEN + jnp.arange(ms, dtype=jnp.int32) % t_new)
    qkv_s, u_s = _project(xs, row(norm_mix_w), w_in_bf, row(q_norm_w), row(k_norm_w), cos_s, sin_s, tm=ms)
    caches = (cache_k_w128, cache_v_w128, cache_k_w512, cache_v_w512, cache_k_w2048, cache_v_w2048)
    o_s, lse_s, s_win = _sample_attn(qkv_s, caches)

    n_state = CONV_TAPS - 1
    ucat = jnp.concatenate(
        [state_conv, u_s.reshape(nbatch, t_new, CONV_CH),
         jnp.zeros((nbatch, SAMPLE_CONV_ROWS - n_state - t_new, CONV_CH), F32)], axis=1)
    wsh = jnp.stack([jnp.pad(conv_w, ((t, SAMPLE_CONV_ROWS - CONV_TAPS - t), (0, 0))) for t in range(t_new)])
    conv_s = _sample_conv(ucat, wsh, *conv_vecs)
    conv_s = conv_s.transpose(1, 0, 2).reshape(ms, CONV_CH)
    s_conv = ucat[:, t_new:t_new + n_state]
    x1_s = _out_proj(xs, [o_s] * N_GROUPS, [lse_s] * N_GROUPS, (0, 1, 2), conv_s, w_out_bf, tm=ms)
    x2_s = _mem_attn_sample(x1_s, row(norm_mem_w), wq_mem_bf, row(qn_mem_w),
                            cache_mem_k.reshape(-1, HEAD_DIM), cache_mem_v.reshape(-1, HEAD_DIM), wo_mem_bf)
    y_s = _peer(x2_s, *peer_w, u_bf, vt_bf)[0]

    return (y_p.reshape(x_prompt.shape), y_s.reshape(x_sample.shape), *p_win, p_conv, kv4(mem_k), kv4(mem_v),
            *s_win, s_conv)
```
